```python
import math
import jax, jax.numpy as jnp
from jax import lax
import numpy as np

D_MODEL = 1024
BATCH = 1
SEQ = 16384
DEPTH = 1

A_HEADS = 8
A_HEAD_DIM = 64
A_WIDTH = A_HEADS * A_HEAD_DIM
IDX_HEADS = 8
IDX_DIM = 64
TOPK_MAX = 256
B_HEADS = 4
B_QK_DIM = 64
B_V_DIM = 2 * B_QK_DIM
B_WIDTH = B_HEADS * B_V_DIM
N_BUCKETS = 32
MAX_DISTANCE = 128
N_BIAS_HEADS = A_HEADS + B_HEADS
PLE_DIM = 256
Q_BLOCK = 128
LN_EPS = 1e-5
RMS_EPS = 1e-5
DEEPNORM_ALPHA = (2 * DEPTH) ** 0.25
DEEPNORM_BETA = (8 * DEPTH) ** -0.25

IN_SIZES = (
    A_WIDTH, A_WIDTH, A_WIDTH, A_WIDTH,
    IDX_HEADS * IDX_DIM, IDX_DIM, IDX_HEADS,
    B_HEADS * 2 * B_QK_DIM, B_HEADS * 2 * B_QK_DIM,
    B_WIDTH, B_WIDTH,
    D_MODEL, D_MODEL,
)
IN_COLS = sum(IN_SIZES)

kernel_name = "hybrid_dsa_diffattn_postnorm_block"


def _split_points():
    pts, acc = [], 0
    for s in IN_SIZES[:-1]:
        acc += s
        pts.append(acc)
    return pts


def _layer_norm(x, g, b):
    xf = x.astype(jnp.float32)
    mu = jnp.mean(xf, axis=-1, keepdims=True)
    var = jnp.mean(jnp.square(xf - mu), axis=-1, keepdims=True)
    return ((xf - mu) * lax.rsqrt(var + LN_EPS) * g.astype(jnp.float32) + b.astype(jnp.float32)).astype(x.dtype)


def _rms_norm(x, g):
    xf = x.astype(jnp.float32)
    ms = jnp.mean(jnp.square(xf), axis=-1, keepdims=True)
    return (xf * lax.rsqrt(ms + RMS_EPS) * g.astype(jnp.float32)).astype(x.dtype)


def _rel_bucket(dist):
    n = jnp.maximum(dist, 0)
    max_exact = N_BUCKETS // 2
    nf = jnp.maximum(n, 1).astype(jnp.float32)
    large = max_exact + (jnp.log(nf / max_exact) / math.log(MAX_DISTANCE / max_exact)
                         * (N_BUCKETS - max_exact)).astype(jnp.int32)
    large = jnp.minimum(large, N_BUCKETS - 1)
    return jnp.where(n < max_exact, n, large)


def _blocks(a):
    b, s = a.shape[:2]
    return jnp.moveaxis(a.reshape((b, s // Q_BLOCK, Q_BLOCK) + a.shape[2:]), 1, 0)


def _unblocks(a):
    a = jnp.moveaxis(a, 0, 1)
    return a.reshape((a.shape[0], a.shape[1] * a.shape[2]) + a.shape[3:])


def _dsa_mixer(q, k, v, qi, ki, wi, bias_by_dist):
    L = q.shape[1]
    topk = min(TOPK_MAX, L // 4)
    key_pos = jnp.arange(L)
    t0s = jnp.arange(L // Q_BLOCK) * Q_BLOCK

    def block(args):
        qb, qib, wib, t0 = args
        tq = t0 + jnp.arange(Q_BLOCK)
        dots = jnp.einsum('bqhd,bsd->bqhs', qib, ki).astype(jnp.float32) * (IDX_DIM ** -0.5)
        score = jnp.einsum('bqh,bqhs->bqs', wib.astype(jnp.float32) * (IDX_HEADS ** -0.5), jax.nn.relu(dots))
        causal = key_pos[None, :] <= tq[:, None]
        score = jnp.where(causal[None], score, -jnp.inf)
        _, idx = lax.top_k(score, topk)
        k_sel = jax.vmap(lambda kk, ii: kk[ii])(k, idx)
        v_sel = jax.vmap(lambda vv, ii: vv[ii])(v, idx)
        dist = tq[None, :, None] - idx
        valid = dist >= 0
        bias = bias_by_dist[jnp.maximum(dist, 0)]
        logits = (jnp.einsum('bqhd,bqkhd->bqhk', qb, k_sel).astype(jnp.float32) * (A_HEAD_DIM ** -0.5)
                  + jnp.swapaxes(bias, -1, -2).astype(jnp.float32))
        logits = jnp.where(valid[:, :, None, :], logits, -jnp.inf)
        probs = jax.nn.softmax(logits, axis=-1)
        return jnp.einsum('bqhk,bqkhd->bqhd', probs.astype(v.dtype), v_sel)

    out = lax.map(block, (_blocks(q), _blocks(qi), _blocks(wi), t0s))
    return _unblocks(out)


def _diff_mixer(q, k, v, lam, bias_by_dist):
    L = q.shape[1]
    key_pos = jnp.arange(L)
    t0s = jnp.arange(L // Q_BLOCK) * Q_BLOCK

    def block(args):
        qb, t0 = args
        tq = t0 + jnp.arange(Q_BLOCK)
        dist = tq[:, None] - key_pos[None, :]
        bias = jnp.moveaxis(bias_by_dist[jnp.maximum(dist, 0)], -1, 0)
        logits = (jnp.einsum('bqhmd,bshmd->bhmqs', qb, k).astype(jnp.float32) * (B_QK_DIM ** -0.5)
                  + bias[None, :, None].astype(jnp.float32))
        logits = jnp.where(dist >= 0, logits, -jnp.inf)
        probs = jax.nn.softmax(logits, axis=-1)
        attn = probs[:, :, 0] - lam * probs[:, :, 1]
        return jnp.einsum('bhqs,bshe->bqhe', attn.astype(v.dtype), v)

    out = lax.map(block, (_blocks(q), t0s))
    return _unblocks(out)


def setup_inputs(seed: int = 0) -> dict:
    key = jax.random.key(seed)
    ks = jax.random.split(key, 14)
    f32 = jnp.float32
    x = jax.random.normal(ks[0], (BATCH, SEQ, D_MODEL), f32)
    p = jax.random.normal(ks[1], (DEPTH, BATCH, SEQ, PLE_DIM), f32)
    w_in = jax.random.normal(ks[2], (DEPTH, D_MODEL, IN_COLS), f32) * D_MODEL ** -0.5
    w_pa = jax.random.normal(ks[3], (DEPTH, A_WIDTH, D_MODEL), f32) * A_WIDTH ** -0.5
    w_pb = jax.random.normal(ks[4], (DEPTH, B_WIDTH, D_MODEL), f32) * B_WIDTH ** -0.5
    w_o = jax.random.normal(ks[5], (DEPTH, D_MODEL, D_MODEL), f32) * (D_MODEL ** -0.5 * DEEPNORM_BETA)
    lambda_qk = jax.random.normal(ks[6], (DEPTH, 4, B_QK_DIM), f32) * 0.1
    subln_w = 1.0 + 0.01 * jax.random.normal(ks[7], (DEPTH, B_V_DIM), f32)
    ln_g = 1.0 + 0.01 * jax.random.normal(ks[8], (DEPTH, D_MODEL), f32)
    ln_b = 0.01 * jax.random.normal(ks[9], (DEPTH, D_MODEL), f32)
    w_ple = jax.random.normal(ks[10], (DEPTH, PLE_DIM, D_MODEL), f32) * (PLE_DIM ** -0.5 * 0.5)
    w_ple_gate = jax.random.normal(ks[11], (DEPTH, D_MODEL, D_MODEL), f32) * D_MODEL ** -0.5
    rel_bias = jax.random.normal(ks[12], (N_BUCKETS, N_BIAS_HEADS), f32) * 0.5
    return {"x": x, "p": p, "w_in": w_in, "w_pa": w_pa, "w_pb": w_pb, "w_o": w_o,
            "lambda_qk": lambda_qk, "subln_w": subln_w, "ln_g": ln_g, "ln_b": ln_b,
            "w_ple": w_ple, "w_ple_gate": w_ple_gate, "rel_bias": rel_bias}


def reference(x, p, w_in, w_pa, w_pb, w_o, lambda_qk, subln_w, ln_g, ln_b, w_ple, w_ple_gate, rel_bias):
    b, L, _ = x.shape
    split_pts = _split_points()
    bias_by_dist = rel_bias[_rel_bucket(jnp.arange(L))]
    bias_a = bias_by_dist[:, :A_HEADS]
    bias_b = bias_by_dist[:, A_HEADS:]
    for i in range(DEPTH):
        h = x @ w_in[i]
        (a_q, a_k, a_v, a_z, i_q, i_k, i_w,
         b_q, b_k, b_v, b_z, g_a, g_b) = jnp.split(h, split_pts, axis=-1)

        o_a = _dsa_mixer(a_q.reshape(b, L, A_HEADS, A_HEAD_DIM),
                         a_k.reshape(b, L, A_HEADS, A_HEAD_DIM),
                         a_v.reshape(b, L, A_HEADS, A_HEAD_DIM),
                         i_q.reshape(b, L, IDX_HEADS, IDX_DIM), i_k, i_w, bias_a)
        y_a = o_a.reshape(b, L, A_WIDTH) * jax.nn.silu(a_z)

        lam_init = 0.8 - 0.6 * math.exp(-0.3 * i)
        lq = lambda_qk[i].astype(jnp.float32)
        lam = jnp.exp(jnp.sum(lq[0] * lq[1])) - jnp.exp(jnp.sum(lq[2] * lq[3])) + lam_init
        o_b = _diff_mixer(b_q.reshape(b, L, B_HEADS, 2, B_QK_DIM),
                          b_k.reshape(b, L, B_HEADS, 2, B_QK_DIM),
                          b_v.reshape(b, L, B_HEADS, B_V_DIM), lam, bias_b)
        o_b = _rms_norm(o_b, subln_w[i]) * (1.0 - lam_init)
        y_b = o_b.reshape(b, L, B_WIDTH) * jax.nn.silu(b_z)

        merged = jax.nn.sigmoid(g_a) * (y_a @ w_pa[i]) + jax.nn.sigmoid(g_b) * (y_b @ w_pb[i])
        mix = merged @ w_o[i]

        x = _layer_norm(DEEPNORM_ALPHA * x + mix, ln_g[i], ln_b[i])
        x = x + jax.nn.sigmoid(x @ w_ple_gate[i]) * (p[i] @ w_ple[i])
    return x
```

```python
import functools
import math

import jax
import jax.numpy as jnp
from jax import lax
from jax.experimental import pallas as pl
from jax.experimental.pallas import tpu as pltpu

D_MODEL = 1024
A_HEADS = 8
A_HEAD_DIM = 64
A_WIDTH = A_HEADS * A_HEAD_DIM
IDX_HEADS = 8
IDX_DIM = 64
TOPK_MAX = 256
B_HEADS = 4
B_QK_DIM = 64
B_MAPS = 2 * B_HEADS
B_V_DIM = 2 * B_QK_DIM
B_WIDTH = B_HEADS * B_V_DIM
N_BUCKETS = 32
MAX_DISTANCE = 128
PLE_DIM = 256
LN_EPS = 1e-5
RMS_EPS = 1e-5
DEPTH = 1
DEEPNORM_ALPHA = (2 * DEPTH) ** 0.25

IN_SIZES = (
    A_WIDTH, A_WIDTH, A_WIDTH, A_WIDTH,
    IDX_HEADS * IDX_DIM, IDX_DIM, IDX_HEADS,
    B_HEADS * 2 * B_QK_DIM, B_HEADS * 2 * B_QK_DIM,
    B_WIDTH, B_WIDTH,
    D_MODEL, D_MODEL,
)

BF16 = jnp.bfloat16
F32 = jnp.float32
I32 = jnp.int32

PROJ_ROWS = 256
TQ = 256
KC = 256
TK = 1024
MERGE_ROWS = 256
VMEM_LIMIT = 56 * 1024 * 1024

NEG = -1e30
INT_MIN = -(2 ** 31)


def _nt_dot(a, b):
    return lax.dot_general(a, b, (((1,), (1,)), ((), ())), preferred_element_type=F32)


def _dot(a, b):
    return jnp.dot(a, b, preferred_element_type=F32)


_N_OUTS = (("a_k", A_WIDTH, BF16), ("b_k", B_WIDTH, BF16), ("a_z", A_WIDTH, F32), ("b_z", B_WIDTH, F32),
           ("g_a", D_MODEL, F32), ("g_b", D_MODEL, F32), ("i_k", IDX_DIM, BF16))
_T_OUTS = (("a_qT", A_WIDTH, BF16, A_HEAD_DIM ** -0.5), ("a_vT", A_WIDTH, BF16, 1.0),
           ("i_qT", IDX_HEADS * IDX_DIM, BF16, 1.0), ("b_qT", B_WIDTH, BF16, B_QK_DIM ** -0.5),
           ("b_vT", B_WIDTH, BF16, 1.0), ("i_wT", IDX_HEADS, F32, 1.0))


def _proj_kernel(x_ref, wn_ref, wt_ref, *out_refs):
    xb = x_ref[...].astype(BF16)
    off = 0
    for k, (_, width, dtype) in enumerate(_N_OUTS):
        out_refs[k][...] = _dot(xb, wn_ref[:, off:off + width]).astype(dtype)
        off += width
    off = 0
    for k, (_, width, dtype, scale) in enumerate(_T_OUTS):
        r = _nt_dot(wt_ref[off:off + width, :], xb)
        if scale != 1.0:
            r = r * scale
        out_refs[len(_N_OUTS) + k][...] = r.astype(dtype)
        off += width


def _project(x2, w):
    L = x2.shape[0]
    pts = [0]
    for s in IN_SIZES:
        pts.append(pts[-1] + s)
    names = ("a_q", "a_k", "a_v", "a_z", "i_q", "i_k", "i_w", "b_q", "b_k", "b_v", "b_z", "g_a", "g_b")
    col = {n: w[:, pts[k]:pts[k + 1]] for k, n in enumerate(names)}
    wn = jnp.concatenate([col[n] for n, _, _ in _N_OUTS], axis=1).astype(BF16)
    wt = jnp.concatenate([col[n[:-1]] for n, _, _, _ in _T_OUTS], axis=1).T.astype(BF16)
    rows = PROJ_ROWS
    out_shape, out_specs = [], []
    for _, width, dtype in _N_OUTS:
        out_shape.append(jax.ShapeDtypeStruct((L, width), dtype))
        out_specs.append(pl.BlockSpec((rows, width), lambda i: (i, 0)))
    for _, width, dtype, _ in _T_OUTS:
        out_shape.append(jax.ShapeDtypeStruct((width, L), dtype))
        out_specs.append(pl.BlockSpec((width, rows), lambda i: (0, i)))
    outs = pl.pallas_call(
        _proj_kernel,
        grid=(L // rows,),
        in_specs=[pl.BlockSpec((rows, D_MODEL), lambda i: (i, 0)),
                  pl.BlockSpec(wn.shape, lambda i: (0, 0)),
                  pl.BlockSpec(wt.shape, lambda i: (0, 0))],
        out_specs=out_specs,
        out_shape=out_shape,
        compiler_params=pltpu.CompilerParams(dimension_semantics=("arbitrary",),
                                             vmem_limit_bytes=VMEM_LIMIT),
        name="proj",
    )(x2, wn, wt)
    res = {n: o for (n, _, _), o in zip(_N_OUTS, outs[:len(_N_OUTS)])}
    res.update({n: o for (n, _, _, _), o in zip(_T_OUTS, outs[len(_N_OUTS):])})
    return res


def _bias_tiles(bias_by_dist, far):
    kk = jnp.arange(KC)[:, None]
    qq = jnp.arange(TQ)[None, :]
    tiles = []
    for d in range(2):
        dist = jnp.maximum(d * KC + qq - kk, 0)
        tiles.append(jnp.moveaxis(bias_by_dist[dist] - far, -1, 0))
    tiles.append(jnp.zeros_like(tiles[0]))
    return jnp.stack(tiles).astype(F32)


def _softmax_step(s, m_ref, l_ref, acc_ref, vt, h, rows):
    m_old = m_ref[h:h + 1, :]
    m_new = jnp.maximum(m_old, jnp.max(s, axis=0, keepdims=True))
    alpha = jnp.exp(m_old - m_new)
    p = jnp.exp(s - m_new)
    l_ref[h:h + 1, :] = alpha * l_ref[h:h + 1, :] + jnp.sum(p, axis=0, keepdims=True)
    r0 = h * rows
    acc_ref[r0:r0 + rows, :] = alpha * acc_ref[r0:r0 + rows, :] + _dot(vt, p.astype(BF16))
    m_ref[h:h + 1, :] = m_new


def _silu(z):
    return z * jax.nn.sigmoid(z)


def _dsa_kernel(iqT_ref, iwT_ref, ik_ref, qT_ref, k_ref, vT_ref, z_ref, bias_ref, y_ref,
                key_scr, thr_scr, jx_scr, m_scr, l_scr, acc_scr, *, topk, seq_len):
    i = pl.program_id(0)
    j = pl.program_id(1)
    sub = TK // KC
    n_bits = max(1, (seq_len - 1).bit_length())

    @pl.when(j == 0)
    def _select():
        t0 = i * TQ
        w = iwT_ref[...] * (IDX_HEADS ** -0.5 * IDX_DIM ** -0.5)
        qcol = t0 + lax.broadcasted_iota(I32, (KC, TQ), 1)
        krow = lax.broadcasted_iota(I32, (KC, TQ), 0)

        def score_chunk(c, carry):
            ks = pl.multiple_of(c * KC, KC)
            kt = ik_ref[pl.ds(ks, KC), :]
            sc = jnp.zeros((KC, TQ), F32)
            for h in range(IDX_HEADS):
                d = _dot(kt, iqT_ref[h * IDX_DIM:(h + 1) * IDX_DIM, :])
                sc = sc + w[h:h + 1, :] * jnp.maximum(d, 0.0)
            sc = jnp.where(sc == 0.0, 0.0, sc)
            bits = lax.bitcast_convert_type(sc, I32)
            key = bits ^ ((bits >> 31) & 0x7FFFFFFF)
            key = jnp.where(ks + krow <= qcol, key, INT_MIN)
            key_scr[pl.ds(ks, KC), :] = key
            return carry

        lax.fori_loop(0, i + 1, score_chunk, 0)

        def count(pred):
            def body(c, acc):
                ks = pl.multiple_of(c * KC, KC)
                blk = key_scr[pl.ds(ks, KC), :]
                return acc + jnp.sum(pred(blk, ks).astype(I32), axis=0, keepdims=True)
            return lax.fori_loop(0, i + 1, body, jnp.zeros((1, TQ), I32))

        c0 = count(lambda blk, ks: blk >= 0)
        thr = jnp.where(c0 >= topk, 0, INT_MIN).astype(I32)

        def bit_step(b, thr):
            cand = thr | (jnp.int32(1) << (30 - b))
            cnt = count(lambda blk, ks: blk >= cand)
            return jnp.where(cnt >= topk, cand, thr)

        thr = lax.fori_loop(0, 31, bit_step, thr)

        cnt_gt = count(lambda blk, ks: blk > thr)
        cnt_eq = count(lambda blk, ks: blk == thr)
        need = topk - cnt_gt
        full = thr != INT_MIN
        jx_scr[...] = jnp.where(full, seq_len, 0).astype(I32)
        trim = full & (cnt_eq > need)

        @pl.when(jnp.max(trim.astype(I32)) > 0)
        def _trim_ties():
            def idx_step(b, jx):
                cand = jx | (jnp.int32(1) << (n_bits - 1 - b))
                cnt = count(lambda blk, ks: (blk == thr) & (ks + krow < cand))
                return jnp.where(cnt < need, cand, jx)
            jx = lax.fori_loop(0, n_bits, idx_step, jnp.zeros((1, TQ), I32))
            jx_scr[...] = jnp.where(trim, jx + 1, jx_scr[...])

        thr_scr[...] = thr
        m_scr[...] = jnp.full(m_scr.shape, NEG, F32)
        l_scr[...] = jnp.zeros(l_scr.shape, F32)
        acc_scr[...] = jnp.zeros(acc_scr.shape, F32)

    thr = thr_scr[...]
    jx = jx_scr[...]
    krow = lax.broadcasted_iota(I32, (KC, TQ), 0)
    for c in range(sub):
        kc = j * sub + c

        @pl.when(kc <= i)
        def _attend(c=c, kc=kc):
            ks = pl.multiple_of(kc * KC, KC)
            key = key_scr[pl.ds(ks, KC), :]
            sel = (key > thr) | ((key == thr) & (ks + krow < jx))
            addmask = jnp.where(sel, 0.0, NEG)
            bidx = jnp.minimum(i - kc, 2)
            for h in range(A_HEADS):
                lo = h * A_HEAD_DIM
                s = _dot(k_ref[c * KC:(c + 1) * KC, lo:lo + A_HEAD_DIM], qT_ref[lo:lo + A_HEAD_DIM, :])
                s = s + bias_ref[bidx, h] + addmask
                _softmax_step(s, m_scr, l_scr, acc_scr, vT_ref[lo:lo + A_HEAD_DIM, c * KC:(c + 1) * KC],
                              h, A_HEAD_DIM)

    @pl.when(j == (i * TQ + TQ - 1) // TK)
    def _finish():
        inv = 1.0 / l_scr[...]
        for h in range(A_HEADS):
            lo = h * A_HEAD_DIM
            acc_scr[lo:lo + A_HEAD_DIM, :] = acc_scr[lo:lo + A_HEAD_DIM, :] * inv[h:h + 1, :]
        o = acc_scr[...].T
        y_ref[...] = (o * _silu(z_ref[...])).astype(y_ref.dtype)


def _dsa_mixer(pr, bias_tiles, L):
    topk = min(TOPK_MAX, L // 4)
    nq, nk = L // TQ, L // TK

    def kv_blk(i, j):
        return jnp.minimum(j, (i * TQ + TQ - 1) // TK)

    return pl.pallas_call(
        functools.partial(_dsa_kernel, topk=topk, seq_len=L),
        grid=(nq, nk),
        in_specs=[
            pl.BlockSpec((IDX_HEADS * IDX_DIM, TQ), lambda i, j: (0, i)),
            pl.BlockSpec((IDX_HEADS, TQ), lambda i, j: (0, i)),
            pl.BlockSpec((L, IDX_DIM), lambda i, j: (0, 0)),
            pl.BlockSpec((A_WIDTH, TQ), lambda i, j: (0, i)),
            pl.BlockSpec((TK, A_WIDTH), lambda i, j: (kv_blk(i, j), 0)),
            pl.BlockSpec((A_WIDTH, TK), lambda i, j: (0, kv_blk(i, j))),
            pl.BlockSpec((TQ, A_WIDTH), lambda i, j: (i, 0)),
            pl.BlockSpec(bias_tiles.shape, lambda i, j: (0, 0, 0, 0)),
        ],
        out_specs=pl.BlockSpec((TQ, A_WIDTH), lambda i, j: (i, 0)),
        out_shape=jax.ShapeDtypeStruct((L, A_WIDTH), BF16),
        scratch_shapes=[
            pltpu.VMEM((L, TQ), I32),
            pltpu.VMEM((1, TQ), I32),
            pltpu.VMEM((1, TQ), I32),
            pltpu.VMEM((A_HEADS, TQ), F32),
            pltpu.VMEM((A_HEADS, TQ), F32),
            pltpu.VMEM((A_WIDTH, TQ), F32),
        ],
        compiler_params=pltpu.CompilerParams(dimension_semantics=("arbitrary", "arbitrary"),
                                             vmem_limit_bytes=VMEM_LIMIT),
        name="dsa",
    )(pr["i_qT"], pr["i_wT"], pr["i_k"], pr["a_qT"], pr["a_k"], pr["a_vT"], pr["a_z"], bias_tiles)


def _diff_kernel(qT_ref, k_ref, vT_ref, z_ref, bias_ref, lam_ref, subw_ref, y_ref,
                 m_scr, l_scr, acc_scr, *, lam_init):
    i = pl.program_id(0)
    j = pl.program_id(1)
    sub = TK // KC

    @pl.when(j == 0)
    def _init():
        m_scr[...] = jnp.full(m_scr.shape, NEG, F32)
        l_scr[...] = jnp.zeros(l_scr.shape, F32)
        acc_scr[...] = jnp.zeros(acc_scr.shape, F32)

    krow = lax.broadcasted_iota(I32, (KC, TQ), 0)
    qcol = lax.broadcasted_iota(I32, (KC, TQ), 1)
    for c in range(sub):
        kc = j * sub + c

        @pl.when(kc <= i)
        def _attend(c=c, kc=kc):
            addmask = jnp.where(kc * KC + krow <= i * TQ + qcol, 0.0, NEG)
            bidx = jnp.minimum(i - kc, 2)
            for mi in range(B_MAPS):
                hb = mi // 2
                lo = mi * B_QK_DIM
                s = _dot(k_ref[c * KC:(c + 1) * KC, lo:lo + B_QK_DIM], qT_ref[lo:lo + B_QK_DIM, :])
                s = s + bias_ref[bidx, hb] + addmask
                _softmax_step(s, m_scr, l_scr, acc_scr,
                              vT_ref[hb * B_V_DIM:(hb + 1) * B_V_DIM, c * KC:(c + 1) * KC], mi, B_V_DIM)

    @pl.when(j == (i * TQ + TQ - 1) // TK)
    def _finish():
        lq = lam_ref[...]
        lam = (jnp.exp(jnp.sum(lq[0:1, :] * lq[1:2, :], axis=1, keepdims=True))
               - jnp.exp(jnp.sum(lq[2:3, :] * lq[3:4, :], axis=1, keepdims=True)) + lam_init)
        inv = 1.0 / l_scr[...]
        for hb in range(B_HEADS):
            r1 = (2 * hb) * B_V_DIM
            r2 = (2 * hb + 1) * B_V_DIM
            o = (acc_scr[r1:r1 + B_V_DIM, :] * inv[2 * hb:2 * hb + 1, :]
                 - lam * (acc_scr[r2:r2 + B_V_DIM, :] * inv[2 * hb + 1:2 * hb + 2, :]))
            ms = jnp.mean(o * o, axis=0, keepdims=True)
            o = o * lax.rsqrt(ms + RMS_EPS) * subw_ref[...] * (1.0 - lam_init)
            acc_scr[hb * B_V_DIM:(hb + 1) * B_V_DIM, :] = o
        o = acc_scr[0:B_WIDTH, :].T
        y_ref[...] = (o * _silu(z_ref[...])).astype(y_ref.dtype)


def _diff_mixer(pr, bias_tiles, lambda_qk, subln_w, lam_init, L):
    nq, nk = L // TQ, L // TK

    def kv_blk(i, j):
        return jnp.minimum(j, (i * TQ + TQ - 1) // TK)

    return pl.pallas_call(
        functools.partial(_diff_kernel, lam_init=lam_init),
        grid=(nq, nk),
        in_specs=[
            pl.BlockSpec((B_WIDTH, TQ), lambda i, j: (0, i)),
            pl.BlockSpec((TK, B_WIDTH), lambda i, j: (kv_blk(i, j), 0)),
            pl.BlockSpec((B_WIDTH, TK), lambda i, j: (0, kv_blk(i, j))),
            pl.BlockSpec((TQ, B_WIDTH), lambda i, j: (i, 0)),
            pl.BlockSpec(bias_tiles.shape, lambda i, j: (0, 0, 0, 0)),
            pl.BlockSpec(lambda_qk.shape, lambda i, j: (0, 0)),
            pl.BlockSpec(subln_w.shape, lambda i, j: (0, 0)),
        ],
        out_specs=pl.BlockSpec((TQ, B_WIDTH), lambda i, j: (i, 0)),
        out_shape=jax.ShapeDtypeStruct((L, B_WIDTH), BF16),
        scratch_shapes=[
            pltpu.VMEM((B_MAPS, TQ), F32),
            pltpu.VMEM((B_MAPS, TQ), F32),
            pltpu.VMEM((B_MAPS * B_V_DIM, TQ), F32),
        ],
        compiler_params=pltpu.CompilerParams(dimension_semantics=("arbitrary", "arbitrary"),
                                             vmem_limit_bytes=VMEM_LIMIT),
        name="diff",
    )(pr["b_qT"], pr["b_k"], pr["b_vT"], pr["b_z"], bias_tiles, lambda_qk, subln_w)


def _merge_kernel(x_ref, p_ref, ya_ref, yb_ref, ga_ref, gb_ref, wpa_ref, wpb_ref, wo_ref,
                  lng_ref, lnb_ref, wple_ref, wgate_ref, out_ref):
    merged = (jax.nn.sigmoid(ga_ref[...]) * _dot(ya_ref[...], wpa_ref[...])
              + jax.nn.sigmoid(gb_ref[...]) * _dot(yb_ref[...], wpb_ref[...]))
    mix = _dot(merged.astype(BF16), wo_ref[...])
    z = DEEPNORM_ALPHA * x_ref[...] + mix
    mu = jnp.mean(z, axis=-1, keepdims=True)
    zc = z - mu
    var = jnp.mean(zc * zc, axis=-1, keepdims=True)
    x1 = zc * lax.rsqrt(var + LN_EPS) * lng_ref[...] + lnb_ref[...]
    gate = jax.nn.sigmoid(_dot(x1.astype(BF16), wgate_ref[...]))
    out_ref[...] = x1 + gate * _dot(p_ref[...].astype(BF16), wple_ref[...])


def _merge(x2, p2, y_a, y_b, g_a, g_b, w_pa, w_pb, w_o, ln_g, ln_b, w_ple, w_gate):
    L = x2.shape[0]
    rows = MERGE_ROWS
    row_blk = lambda width: pl.BlockSpec((rows, width), lambda i: (i, 0))
    full = lambda a: pl.BlockSpec(a.shape, lambda i: (0, 0))
    weights = (w_pa.astype(BF16), w_pb.astype(BF16), w_o.astype(BF16),
               ln_g.reshape(1, D_MODEL), ln_b.reshape(1, D_MODEL), w_ple.astype(BF16), w_gate.astype(BF16))
    return pl.pallas_call(
        _merge_kernel,
        grid=(L // rows,),
        in_specs=[row_blk(D_MODEL), row_blk(PLE_DIM), row_blk(A_WIDTH), row_blk(B_WIDTH),
                  row_blk(D_MODEL), row_blk(D_MODEL)] + [full(a) for a in weights],
        out_specs=row_blk(D_MODEL),
        out_shape=jax.ShapeDtypeStruct((L, D_MODEL), F32),
        compiler_params=pltpu.CompilerParams(dimension_semantics=("arbitrary",),
                                             vmem_limit_bytes=VMEM_LIMIT),
        name="merge",
    )(x2, p2, y_a, y_b, g_a, g_b, *weights)


def _rel_bucket(dist):
    n = jnp.maximum(dist, 0)
    max_exact = N_BUCKETS // 2
    nf = jnp.maximum(n, 1).astype(F32)
    large = max_exact + (jnp.log(nf / max_exact) / math.log(MAX_DISTANCE / max_exact)
                         * (N_BUCKETS - max_exact)).astype(I32)
    large = jnp.minimum(large, N_BUCKETS - 1)
    return jnp.where(n < max_exact, n, large)


def kernel(x, p, w_in, w_pa, w_pb, w_o, lambda_qk, subln_w, ln_g, ln_b, w_ple, w_ple_gate, rel_bias):
    b, L, _ = x.shape
    assert b == 1 and w_in.shape[0] == DEPTH == 1
    assert L % TK == 0 and KC >= MAX_DISTANCE
    bias_by_dist = rel_bias[_rel_bucket(jnp.arange(2 * KC))]
    far = rel_bias[N_BUCKETS - 1]
    tiles = _bias_tiles(bias_by_dist, far)
    x2 = x[0]
    pr = _project(x2, w_in[0])
    y_a = _dsa_mixer(pr, tiles[:, :A_HEADS], L)
    lam_init = 0.8 - 0.6 * math.exp(-0.3 * 0)
    y_b = _diff_mixer(pr, tiles[:, A_HEADS:], lambda_qk[0], subln_w[0].reshape(B_V_DIM, 1), lam_init, L)
    out = _merge(x2, p[0, 0], y_a, y_b, pr["g_a"], pr["g_b"], w_pa[0], w_pb[0], w_o[0],
                 ln_g[0], ln_b[0], w_ple[0], w_ple_gate[0])
    return out[None]
```

```python
import functools
import math

import jax
import jax.numpy as jnp
from jax import lax
from jax.experimental import pallas as pl
from jax.experimental.pallas import tpu as pltpu

D_MODEL = 1024
A_HEADS = 8
A_HEAD_DIM = 64
A_WIDTH = A_HEADS * A_HEAD_DIM
IDX_HEADS = 8
IDX_DIM = 64
TOPK_MAX = 256
B_HEADS = 4
B_QK_DIM = 64
B_MAPS = 2 * B_HEADS
B_V_DIM = 2 * B_QK_DIM
B_WIDTH = B_HEADS * B_V_DIM
N_BUCKETS = 32
MAX_DISTANCE = 128
PLE_DIM = 256
LN_EPS = 1e-5
RMS_EPS = 1e-5
DEPTH = 1
DEEPNORM_ALPHA = (2 * DEPTH) ** 0.25
LOG2E = math.log2(math.e)

IN_SIZES = (
    A_WIDTH, A_WIDTH, A_WIDTH, A_WIDTH,
    IDX_HEADS * IDX_DIM, IDX_DIM, IDX_HEADS,
    B_HEADS * 2 * B_QK_DIM, B_HEADS * 2 * B_QK_DIM,
    B_WIDTH, B_WIDTH,
    D_MODEL, D_MODEL,
)

BF16 = jnp.bfloat16
F32 = jnp.float32
I32 = jnp.int32

TQ = 256
KC = 256
SUB = 4
TK = SUB * KC
PROJ_ROWS = KC
MERGE_ROWS = 256
VMEM_LIMIT = 56 * 1024 * 1024
BF16_ROWS = 16
F32_ROWS = 8

NEG = -1e30
INT_MIN = -(2 ** 31)


def _nt_dot(a, b):
    return lax.dot_general(a, b, (((1,), (1,)), ((), ())), preferred_element_type=F32)


def _dot(a, b):
    return jnp.dot(a, b, preferred_element_type=F32)


def _resident(shape):
    return pl.BlockSpec(shape, lambda *_: (0,) * len(shape), pipeline_mode=pl.Buffered(1))


_N_OUTS = (("a_k", A_WIDTH, BF16), ("b_k", B_WIDTH, BF16), ("a_z", A_WIDTH, F32), ("b_z", B_WIDTH, F32),
           ("g_a", D_MODEL, F32), ("g_b", D_MODEL, F32), ("i_k", IDX_DIM, BF16))
_T_OUTS = (("a_qT", A_WIDTH, BF16, A_HEAD_DIM ** -0.5 * LOG2E, False), ("a_vT", A_WIDTH, BF16, 1.0, True),
           ("i_qT", IDX_HEADS * IDX_DIM, BF16, 1.0, False), ("b_qT", B_WIDTH, BF16, B_QK_DIM ** -0.5 * LOG2E, False),
           ("b_vT", B_WIDTH, BF16, 1.0, True), ("i_wT", IDX_HEADS, F32, 1.0, False))


def _proj_kernel(x_ref, wn_ref, wt_ref, *out_refs):
    xb = x_ref[...].astype(BF16)
    off = 0
    for k, (_, width, dtype) in enumerate(_N_OUTS):
        out_refs[k][...] = _dot(xb, wn_ref[:, off:off + width]).astype(dtype)
        off += width
    off = 0
    for k, (_, width, dtype, scale, chunked) in enumerate(_T_OUTS):
        r = _nt_dot(wt_ref[off:off + width, :], xb)
        if scale != 1.0:
            r = r * scale
        o_ref = out_refs[len(_N_OUTS) + k]
        if chunked:
            o_ref[0] = r.astype(dtype)
        else:
            o_ref[...] = r.astype(dtype)
        off += width


def _project(x2, w):
    L = x2.shape[0]
    pts = [0]
    for s in IN_SIZES:
        pts.append(pts[-1] + s)
    names = ("a_q", "a_k", "a_v", "a_z", "i_q", "i_k", "i_w", "b_q", "b_k", "b_v", "b_z", "g_a", "g_b")
    col = {n: w[:, pts[k]:pts[k + 1]] for k, n in enumerate(names)}
    wn = jnp.concatenate([col[n] for n, _, _ in _N_OUTS], axis=1).astype(BF16)
    wt = jnp.concatenate([col[n[:-1]] for n, _, _, _, _ in _T_OUTS], axis=1).T.astype(BF16)
    rows = PROJ_ROWS
    out_shape, out_specs = [], []
    for _, width, dtype in _N_OUTS:
        out_shape.append(jax.ShapeDtypeStruct((L, width), dtype))
        out_specs.append(pl.BlockSpec((rows, width), lambda i: (i, 0)))
    for _, width, dtype, _, chunked in _T_OUTS:
        if chunked:
            out_shape.append(jax.ShapeDtypeStruct((L // rows, width, rows), dtype))
            out_specs.append(pl.BlockSpec((1, width, rows), lambda i: (i, 0, 0)))
        else:
            out_shape.append(jax.ShapeDtypeStruct((width, L), dtype))
            out_specs.append(pl.BlockSpec((width, rows), lambda i: (0, i)))
    outs = pl.pallas_call(
        _proj_kernel,
        grid=(L // rows,),
        in_specs=[pl.BlockSpec((rows, D_MODEL), lambda i: (i, 0)), _resident(wn.shape), _resident(wt.shape)],
        out_specs=out_specs,
        out_shape=out_shape,
        compiler_params=pltpu.CompilerParams(dimension_semantics=("arbitrary",),
                                             vmem_limit_bytes=VMEM_LIMIT),
        name="proj",
    )(x2, wn, wt)
    res = {n: o for (n, _, _), o in zip(_N_OUTS, outs[:len(_N_OUTS)])}
    res.update({n: o for (n, _, _, _, _), o in zip(_T_OUTS, outs[len(_N_OUTS):])})
    return res


def _rel_bucket(dist):
    n = jnp.maximum(dist, 0)
    max_exact = N_BUCKETS // 2
    nf = jnp.maximum(n, 1).astype(F32)
    large = max_exact + (jnp.log(nf / max_exact) / math.log(MAX_DISTANCE / max_exact)
                         * (N_BUCKETS - max_exact)).astype(I32)
    large = jnp.minimum(large, N_BUCKETS - 1)
    return jnp.where(n < max_exact, n, large)


def _bias_tiles(rel_bias):
    kk = jnp.arange(KC)[:, None]
    qq = jnp.arange(TQ)[None, :]
    shifted = rel_bias - rel_bias[N_BUCKETS - 1]
    tiles = []
    for d in range(2):
        onehot = jax.nn.one_hot(_rel_bucket(d * KC + qq - kk), N_BUCKETS, dtype=F32)
        tiles.append(jnp.einsum("kqb,bh->hkq", onehot, shifted, precision=lax.Precision.HIGHEST))
    tiles.append(jnp.zeros_like(tiles[0]))
    return (jnp.stack(tiles) * LOG2E).astype(F32)


def _stage_logits(k_ref, qT_ref, s_scr, buf, c, n_streams, dim):
    ks = c * KC if isinstance(c, int) else pl.multiple_of(c * KC, KC)
    for h in range(n_streams):
        lo = h * dim
        s_scr[buf, h] = _dot(k_ref[pl.ds(ks, KC), lo:lo + dim], qT_ref[lo:lo + dim, :])


def _softmax_step(s, m_ref, acc_ref, vt, h):
    rows = acc_ref.shape[1]
    m_old = m_ref[h]
    m_new = jnp.maximum(m_old, jnp.max(s, axis=0, keepdims=True))
    alpha = jnp.exp2(m_old - m_new)
    p = jnp.exp2(s - m_new).astype(BF16)
    vt_ext = jnp.concatenate([vt, jnp.ones((BF16_ROWS, KC), BF16)], axis=0)
    acc_ref[h] = alpha * acc_ref[h] + _dot(vt_ext, p)[:rows]
    m_ref[h] = m_new


def _silu(z):
    return z * jax.nn.sigmoid(z)


def _kv_block(i, j):
    return jnp.minimum(j, i // SUB)


def _dsa_kernel(iqT_ref, iwT_ref, ik_ref, qT_ref, k_ref, vT_ref, z_ref, bias_ref, y_ref,
                key_scr, thr_scr, jx_scr, m_scr, acc_scr, s_scr, *, topk, seq_len):
    i = pl.program_id(0)
    j = pl.program_id(1)
    n_bits = max(1, (seq_len - 1).bit_length())
    krow = lax.broadcasted_iota(I32, (KC, TQ), 0)

    @pl.when(j == 0)
    def _select():
        t0 = i * TQ
        w = iwT_ref[...] * (IDX_HEADS ** -0.5 * IDX_DIM ** -0.5)
        qcol = t0 + lax.broadcasted_iota(I32, (KC, TQ), 1)

        def score_chunk(c, carry):
            ks = pl.multiple_of(c * KC, KC)
            kt = ik_ref[pl.ds(ks, KC), :]
            sc = jnp.zeros((KC, TQ), F32)
            for h in range(IDX_HEADS):
                d = _dot(kt, iqT_ref[h * IDX_DIM:(h + 1) * IDX_DIM, :])
                sc = sc + w[h:h + 1, :] * jnp.maximum(d, 0.0)
            sc = jnp.where(sc == 0.0, 0.0, sc)
            bits = lax.bitcast_convert_type(sc, I32)
            key = bits ^ ((bits >> 31) & 0x7FFFFFFF)
            key = jnp.where(ks + krow <= qcol, key, INT_MIN)
            key_scr[pl.ds(ks, KC), :] = key
            return carry

        lax.fori_loop(0, i + 1, score_chunk, 0)

        def count(pred):
            def body(c, acc):
                ks = pl.multiple_of(c * KC, KC)
                blk = key_scr[pl.ds(ks, KC), :]
                return acc + jnp.sum(pred(blk, ks).astype(I32), axis=0, keepdims=True)
            return lax.fori_loop(0, i + 1, body, jnp.zeros((1, TQ), I32))

        c0 = count(lambda blk, ks: blk >= 0)
        thr = jnp.where(c0 >= topk, 0, INT_MIN).astype(I32)

        def bit_step(b, thr):
            cand = thr | (jnp.int32(1) << (30 - b))
            cnt = count(lambda blk, ks: blk >= cand)
            return jnp.where(cnt >= topk, cand, thr)

        thr = lax.fori_loop(0, 31, bit_step, thr)

        cnt_gt = count(lambda blk, ks: blk > thr)
        cnt_eq = count(lambda blk, ks: blk == thr)
        need = topk - cnt_gt
        full = thr != INT_MIN
        jx_scr[...] = jnp.where(full, seq_len, 0).astype(I32)
        trim = full & (cnt_eq > need)

        @pl.when(jnp.max(trim.astype(I32)) > 0)
        def _trim_ties():
            def idx_step(b, jx):
                cand = jx | (jnp.int32(1) << (n_bits - 1 - b))
                cnt = count(lambda blk, ks: (blk == thr) & (ks + krow < cand))
                return jnp.where(cnt < need, cand, jx)
            jx = lax.fori_loop(0, n_bits, idx_step, jnp.zeros((1, TQ), I32))
            jx_scr[...] = jnp.where(trim, jx + 1, jx_scr[...])

        thr_scr[...] = thr
        m_scr[...] = jnp.full(m_scr.shape, NEG, F32)
        acc_scr[...] = jnp.zeros(acc_scr.shape, F32)

    thr = thr_scr[...]
    jx = jx_scr[...]

    def consume(buf, c, bidx):
        ks = (j * SUB + c) * KC
        key = key_scr[pl.ds(pl.multiple_of(ks, KC), KC), :]
        sel = (key > thr) | ((key == thr) & (ks + krow < jx))
        addmask = jnp.where(sel, 0.0, NEG)
        for h in range(A_HEADS):
            lo = h * A_HEAD_DIM
            s = s_scr[buf, h] + addmask
            if bidx is not None:
                s = s + bias_ref[bidx, h]
            _softmax_step(s, m_scr, acc_scr, vT_ref[c, lo:lo + A_HEAD_DIM, :], h)

    d = i - j * SUB

    @pl.when(d > SUB)
    def _far_step():
        _stage_logits(k_ref, qT_ref, s_scr, 0, 0, A_HEADS, A_HEAD_DIM)
        for c in range(SUB):
            if c + 1 < SUB:
                _stage_logits(k_ref, qT_ref, s_scr, (c + 1) % 2, c + 1, A_HEADS, A_HEAD_DIM)
            consume(c % 2, c, None)

    @pl.when((d >= 0) & (d <= SUB))
    def _edge_step():
        def body(c, carry):
            _stage_logits(k_ref, qT_ref, s_scr, 0, c, A_HEADS, A_HEAD_DIM)
            consume(0, c, jnp.minimum(d - c, 2))
            return carry
        lax.fori_loop(0, jnp.minimum(d + 1, SUB), body, 0)

    @pl.when(j == i // SUB)
    def _finish():
        outs = []
        for h in range(A_HEADS):
            acc = acc_scr[h]
            outs.append(acc[:A_HEAD_DIM] * (1.0 / acc[A_HEAD_DIM:A_HEAD_DIM + 1]))
        o = jnp.concatenate(outs, axis=0).T
        y_ref[...] = (o * _silu(z_ref[...])).astype(y_ref.dtype)


def _dsa_mixer(pr, bias_tiles, L):
    topk = min(TOPK_MAX, L // 4)
    nq, nk = L // TQ, L // TK
    return pl.pallas_call(
        functools.partial(_dsa_kernel, topk=topk, seq_len=L),
        grid=(nq, nk),
        in_specs=[
            pl.BlockSpec((IDX_HEADS * IDX_DIM, TQ), lambda i, j: (0, i)),
            pl.BlockSpec((IDX_HEADS, TQ), lambda i, j: (0, i)),
            _resident((L, IDX_DIM)),
            pl.BlockSpec((A_WIDTH, TQ), lambda i, j: (0, i)),
            pl.BlockSpec((TK, A_WIDTH), lambda i, j: (_kv_block(i, j), 0)),
            pl.BlockSpec((SUB, A_WIDTH, KC), lambda i, j: (_kv_block(i, j), 0, 0)),
            pl.BlockSpec((TQ, A_WIDTH), lambda i, j: (i, 0)),
            _resident(bias_tiles.shape),
        ],
        out_specs=pl.BlockSpec((TQ, A_WIDTH), lambda i, j: (i, 0)),
        out_shape=jax.ShapeDtypeStruct((L, A_WIDTH), BF16),
        scratch_shapes=[
            pltpu.VMEM((L, TQ), I32),
            pltpu.VMEM((1, TQ), I32),
            pltpu.VMEM((1, TQ), I32),
            pltpu.VMEM((A_HEADS, 1, TQ), F32),
            pltpu.VMEM((A_HEADS, A_HEAD_DIM + F32_ROWS, TQ), F32),
            pltpu.VMEM((2, A_HEADS, KC, TQ), F32),
        ],
        compiler_params=pltpu.CompilerParams(dimension_semantics=("arbitrary", "arbitrary"),
                                             vmem_limit_bytes=VMEM_LIMIT),
        name="dsa",
    )(pr["i_qT"], pr["i_wT"], pr["i_k"], pr["a_qT"], pr["a_k"], pr["a_vT"], pr["a_z"], bias_tiles)


def _diff_kernel(qT_ref, k_ref, vT_ref, z_ref, bias_ref, lam_ref, subw_ref, y_ref,
                 m_scr, acc_scr, s_scr, *, lam_init):
    i = pl.program_id(0)
    j = pl.program_id(1)

    @pl.when(j == 0)
    def _init():
        m_scr[...] = jnp.full(m_scr.shape, NEG, F32)
        acc_scr[...] = jnp.zeros(acc_scr.shape, F32)

    def consume(buf, c, edge):
        if edge is not None:
            bidx, addmask = edge
        for mi in range(B_MAPS):
            hb = mi // 2
            s = s_scr[buf, mi]
            if edge is not None:
                s = s + bias_ref[bidx, hb] + addmask
            _softmax_step(s, m_scr, acc_scr, vT_ref[c, hb * B_V_DIM:(hb + 1) * B_V_DIM, :], mi)

    d = i - j * SUB

    @pl.when(d > SUB)
    def _far_step():
        _stage_logits(k_ref, qT_ref, s_scr, 0, 0, B_MAPS, B_QK_DIM)
        for c in range(SUB):
            if c + 1 < SUB:
                _stage_logits(k_ref, qT_ref, s_scr, (c + 1) % 2, c + 1, B_MAPS, B_QK_DIM)
            consume(c % 2, c, None)

    @pl.when((d >= 0) & (d <= SUB))
    def _edge_step():
        krow = lax.broadcasted_iota(I32, (KC, TQ), 0)
        qcol = lax.broadcasted_iota(I32, (KC, TQ), 1)

        def body(c, carry):
            _stage_logits(k_ref, qT_ref, s_scr, 0, c, B_MAPS, B_QK_DIM)
            addmask = jnp.where(krow <= (d - c) * KC + qcol, 0.0, NEG)
            consume(0, c, (jnp.minimum(d - c, 2), addmask))
            return carry
        lax.fori_loop(0, jnp.minimum(d + 1, SUB), body, 0)

    @pl.when(j == i // SUB)
    def _finish():
        lq = lam_ref[...]
        lam = (jnp.exp(jnp.sum(lq[0:1, :] * lq[1:2, :], axis=1, keepdims=True))
               - jnp.exp(jnp.sum(lq[2:3, :] * lq[3:4, :], axis=1, keepdims=True)) + lam_init)
        outs = []
        for hb in range(B_HEADS):
            a1 = acc_scr[2 * hb]
            a2 = acc_scr[2 * hb + 1]
            o = (a1[:B_V_DIM] * (1.0 / a1[B_V_DIM:B_V_DIM + 1])
                 - lam * (a2[:B_V_DIM] * (1.0 / a2[B_V_DIM:B_V_DIM + 1])))
            ms = jnp.mean(o * o, axis=0, keepdims=True)
            outs.append(o * lax.rsqrt(ms + RMS_EPS) * subw_ref[...] * (1.0 - lam_init))
        o = jnp.concatenate(outs, axis=0).T
        y_ref[...] = (o * _silu(z_ref[...])).astype(y_ref.dtype)


def _diff_mixer(pr, bias_tiles, lambda_qk, subln_w, lam_init, L):
    nq, nk = L // TQ, L // TK
    return pl.pallas_call(
        functools.partial(_diff_kernel, lam_init=lam_init),
        grid=(nq, nk),
        in_specs=[
            pl.BlockSpec((B_WIDTH, TQ), lambda i, j: (0, i)),
            pl.BlockSpec((TK, B_WIDTH), lambda i, j: (_kv_block(i, j), 0)),
            pl.BlockSpec((SUB, B_WIDTH, KC), lambda i, j: (_kv_block(i, j), 0, 0)),
            pl.BlockSpec((TQ, B_WIDTH), lambda i, j: (i, 0)),
            _resident(bias_tiles.shape),
            _resident(lambda_qk.shape),
            _resident(subln_w.shape),
        ],
        out_specs=pl.BlockSpec((TQ, B_WIDTH), lambda i, j: (i, 0)),
        out_shape=jax.ShapeDtypeStruct((L, B_WIDTH), BF16),
        scratch_shapes=[
            pltpu.VMEM((B_MAPS, 1, TQ), F32),
            pltpu.VMEM((B_MAPS, B_V_DIM + F32_ROWS, TQ), F32),
            pltpu.VMEM((2, B_MAPS, KC, TQ), F32),
        ],
        compiler_params=pltpu.CompilerParams(dimension_semantics=("arbitrary", "arbitrary"),
                                             vmem_limit_bytes=VMEM_LIMIT),
        name="diff",
    )(pr["b_qT"], pr["b_k"], pr["b_vT"], pr["b_z"], bias_tiles, lambda_qk, subln_w)


def _merge_kernel(x_ref, p_ref, ya_ref, yb_ref, ga_ref, gb_ref, wpa_ref, wpb_ref, wo_ref,
                  lng_ref, lnb_ref, wple_ref, wgate_ref, out_ref):
    merged = (jax.nn.sigmoid(ga_ref[...]) * _dot(ya_ref[...], wpa_ref[...])
              + jax.nn.sigmoid(gb_ref[...]) * _dot(yb_ref[...], wpb_ref[...]))
    mix = _dot(merged.astype(BF16), wo_ref[...])
    z = DEEPNORM_ALPHA * x_ref[...] + mix
    mu = jnp.mean(z, axis=-1, keepdims=True)
    zc = z - mu
    var = jnp.mean(zc * zc, axis=-1, keepdims=True)
    x1 = zc * lax.rsqrt(var + LN_EPS) * lng_ref[...] + lnb_ref[...]
    gate = jax.nn.sigmoid(_dot(x1.astype(BF16), wgate_ref[...]))
    out_ref[...] = x1 + gate * _dot(p_ref[...].astype(BF16), wple_ref[...])


def _merge(x2, p2, y_a, y_b, g_a, g_b, w_pa, w_pb, w_o, ln_g, ln_b, w_ple, w_gate):
    L = x2.shape[0]
    rows = MERGE_ROWS
    row_blk = lambda width: pl.BlockSpec((rows, width), lambda i: (i, 0))
    weights = (w_pa.astype(BF16), w_pb.astype(BF16), w_o.astype(BF16),
               ln_g.reshape(1, D_MODEL), ln_b.reshape(1, D_MODEL), w_ple.astype(BF16), w_gate.astype(BF16))
    return pl.pallas_call(
        _merge_kernel,
        grid=(L // rows,),
        in_specs=[row_blk(D_MODEL), row_blk(PLE_DIM), row_blk(A_WIDTH), row_blk(B_WIDTH),
                  row_blk(D_MODEL), row_blk(D_MODEL)] + [_resident(a.shape) for a in weights],
        out_specs=row_blk(D_MODEL),
        out_shape=jax.ShapeDtypeStruct((L, D_MODEL), F32),
        compiler_params=pltpu.CompilerParams(dimension_semantics=("arbitrary",),
                                             vmem_limit_bytes=VMEM_LIMIT),
        name="merge",
    )(x2, p2, y_a, y_b, g_a, g_b, *weights)


def kernel(x, p, w_in, w_pa, w_pb, w_o, lambda_qk, subln_w, ln_g, ln_b, w_ple, w_ple_gate, rel_bias):
    b, L, _ = x.shape
    assert b == 1 and w_in.shape[0] == DEPTH == 1
    assert L % TK == 0 and TQ == KC and KC >= MAX_DISTANCE
    tiles = _bias_tiles(rel_bias)
    x2 = x[0]
    pr = _project(x2, w_in[0])
    y_a = _dsa_mixer(pr, tiles[:, :A_HEADS], L)
    lam_init = 0.8 - 0.6 * math.exp(-0.3 * 0)
    y_b = _diff_mixer(pr, tiles[:, A_HEADS:], lambda_qk[0], subln_w[0].reshape(B_V_DIM, 1), lam_init, L)
    out = _merge(x2, p[0, 0], y_a, y_b, pr["g_a"], pr["g_b"], w_pa[0], w_pb[0], w_o[0],
                 ln_g[0], ln_b[0], w_ple[0], w_ple_gate[0])
    return out[None]
```

```python
import functools
import math

import jax
import jax.numpy as jnp
from jax import lax
from jax.experimental import pallas as pl
from jax.experimental.pallas import tpu as pltpu

D_MODEL = 1024
A_HEADS = 8
A_HEAD_DIM = 64
A_WIDTH = A_HEADS * A_HEAD_DIM
IDX_HEADS = 8
IDX_DIM = 64
TOPK_MAX = 256
B_HEADS = 4
B_QK_DIM = 64
B_MAPS = 2 * B_HEADS
B_V_DIM = 2 * B_QK_DIM
B_WIDTH = B_HEADS * B_V_DIM
N_BUCKETS = 32
MAX_DISTANCE = 128
PLE_DIM = 256
LN_EPS = 1e-5
RMS_EPS = 1e-5
DEPTH = 1
DEEPNORM_ALPHA = (2 * DEPTH) ** 0.25
LOG2E = math.log2(math.e)

IN_SIZES = (
    A_WIDTH, A_WIDTH, A_WIDTH, A_WIDTH,
    IDX_HEADS * IDX_DIM, IDX_DIM, IDX_HEADS,
    B_HEADS * 2 * B_QK_DIM, B_HEADS * 2 * B_QK_DIM,
    B_WIDTH, B_WIDTH,
    D_MODEL, D_MODEL,
)

BF16 = jnp.bfloat16
F32 = jnp.float32
I32 = jnp.int32

TQ = 256
KC = 256
SUB = 4
TK = SUB * KC
PROJ_ROWS = KC
MERGE_ROWS = 256
VMEM_LIMIT = 56 * 1024 * 1024
BF16_ROWS = 16
F32_ROWS = 8
WORD_BITS = 32
RADIX_GROUP = 4

NEG = -1e30
INT_MIN = -(2 ** 31)


def _nt_dot(a, b):
    return lax.dot_general(a, b, (((1,), (1,)), ((), ())), preferred_element_type=F32)


def _dot(a, b):
    return jnp.dot(a, b, preferred_element_type=F32)


def _order_key(bits):
    return bits ^ ((bits >> 31) & 0x7FFFFFFF)


def _bit_transpose32(words):
    w = list(words)
    shift, mask = 16, 0x0000FFFF
    while shift:
        for k in range(WORD_BITS):
            if not k & shift:
                t = (w[k] ^ (w[k + shift] >> shift)) & mask
                w[k] = w[k] ^ t
                w[k + shift] = w[k + shift] ^ (t << shift)
        shift >>= 1
        mask = (mask ^ (mask << shift)) & 0xFFFFFFFF
    return w


def _rows_below(limit):
    sub = lax.broadcasted_iota(I32, (F32_ROWS, TQ), 0)
    n = jnp.clip((limit - sub + (F32_ROWS - 1)) >> 3, 0, WORD_BITS)
    return jnp.where(n > 0, jnp.int32(INT_MIN) >> (jnp.maximum(n, 1) - 1), 0)


def _resident(shape):
    return pl.BlockSpec(shape, lambda *_: (0,) * len(shape), pipeline_mode=pl.Buffered(1))


_N_OUTS = (("a_k", A_WIDTH, BF16), ("b_k", B_WIDTH, BF16), ("a_z", A_WIDTH, F32), ("b_z", B_WIDTH, F32),
           ("g_a", D_MODEL, F32), ("g_b", D_MODEL, F32), ("i_k", IDX_DIM, BF16))
_T_OUTS = (("a_qT", A_WIDTH, BF16, A_HEAD_DIM ** -0.5 * LOG2E, False), ("a_vT", A_WIDTH, BF16, 1.0, True),
           ("i_qT", IDX_HEADS * IDX_DIM, BF16, 1.0, False), ("b_qT", B_WIDTH, BF16, B_QK_DIM ** -0.5 * LOG2E, False),
           ("b_vT", B_WIDTH, BF16, 1.0, True), ("i_wT", IDX_HEADS, F32, 1.0, False))


def _proj_kernel(x_ref, wn_ref, wt_ref, *out_refs):
    xb = x_ref[...].astype(BF16)
    off = 0
    for k, (_, width, dtype) in enumerate(_N_OUTS):
        out_refs[k][...] = _dot(xb, wn_ref[:, off:off + width]).astype(dtype)
        off += width
    off = 0
    for k, (_, width, dtype, scale, chunked) in enumerate(_T_OUTS):
        r = _nt_dot(wt_ref[off:off + width, :], xb)
        if scale != 1.0:
            r = r * scale
        o_ref = out_refs[len(_N_OUTS) + k]
        if chunked:
            o_ref[0] = r.astype(dtype)
        else:
            o_ref[...] = r.astype(dtype)
        off += width


def _project(x2, w):
    L = x2.shape[0]
    pts = [0]
    for s in IN_SIZES:
        pts.append(pts[-1] + s)
    names = ("a_q", "a_k", "a_v", "a_z", "i_q", "i_k", "i_w", "b_q", "b_k", "b_v", "b_z", "g_a", "g_b")
    col = {n: w[:, pts[k]:pts[k + 1]] for k, n in enumerate(names)}
    wn = jnp.concatenate([col[n] for n, _, _ in _N_OUTS], axis=1).astype(BF16)
    wt = jnp.concatenate([col[n[:-1]] for n, _, _, _, _ in _T_OUTS], axis=1).T.astype(BF16)
    rows = PROJ_ROWS
    out_shape, out_specs = [], []
    for _, width, dtype in _N_OUTS:
        out_shape.append(jax.ShapeDtypeStruct((L, width), dtype))
        out_specs.append(pl.BlockSpec((rows, width), lambda i: (i, 0)))
    for _, width, dtype, _, chunked in _T_OUTS:
        if chunked:
            out_shape.append(jax.ShapeDtypeStruct((L // rows, width, rows), dtype))
            out_specs.append(pl.BlockSpec((1, width, rows), lambda i: (i, 0, 0)))
        else:
            out_shape.append(jax.ShapeDtypeStruct((width, L), dtype))
            out_specs.append(pl.BlockSpec((width, rows), lambda i: (0, i)))
    outs = pl.pallas_call(
        _proj_kernel,
        grid=(L // rows,),
        in_specs=[pl.BlockSpec((rows, D_MODEL), lambda i: (i, 0)), _resident(wn.shape), _resident(wt.shape)],
        out_specs=out_specs,
        out_shape=out_shape,
        compiler_params=pltpu.CompilerParams(dimension_semantics=("arbitrary",),
                                             vmem_limit_bytes=VMEM_LIMIT),
        name="proj",
    )(x2, wn, wt)
    res = {n: o for (n, _, _), o in zip(_N_OUTS, outs[:len(_N_OUTS)])}
    res.update({n: o for (n, _, _, _, _), o in zip(_T_OUTS, outs[len(_N_OUTS):])})
    return res


def _rel_bucket(dist):
    n = jnp.maximum(dist, 0)
    max_exact = N_BUCKETS // 2
    nf = jnp.maximum(n, 1).astype(F32)
    large = max_exact + (jnp.log(nf / max_exact) / math.log(MAX_DISTANCE / max_exact)
                         * (N_BUCKETS - max_exact)).astype(I32)
    large = jnp.minimum(large, N_BUCKETS - 1)
    return jnp.where(n < max_exact, n, large)


def _bias_tiles(rel_bias):
    kk = jnp.arange(KC)[:, None]
    qq = jnp.arange(TQ)[None, :]
    shifted = rel_bias - rel_bias[N_BUCKETS - 1]
    tiles = []
    for d in range(2):
        onehot = jax.nn.one_hot(_rel_bucket(d * KC + qq - kk), N_BUCKETS, dtype=F32)
        tiles.append(jnp.einsum("kqb,bh->hkq", onehot, shifted, precision=lax.Precision.HIGHEST))
    tiles.append(jnp.zeros_like(tiles[0]))
    return (jnp.stack(tiles) * LOG2E).astype(F32)


def _stage_logits(k_ref, qT_ref, s_scr, buf, c, n_streams, dim):
    ks = c * KC if isinstance(c, int) else pl.multiple_of(c * KC, KC)
    for h in range(n_streams):
        lo = h * dim
        s_scr[buf, h] = _dot(k_ref[pl.ds(ks, KC), lo:lo + dim], qT_ref[lo:lo + dim, :])


def _softmax_step(s, m_ref, acc_ref, vt, h):
    rows = acc_ref.shape[1]
    m_old = m_ref[h]
    m_new = jnp.maximum(m_old, jnp.max(s, axis=0, keepdims=True))
    alpha = jnp.exp2(m_old - m_new)
    p = jnp.exp2(s - m_new).astype(BF16)
    vt_ext = jnp.concatenate([vt, jnp.ones((BF16_ROWS, KC), BF16)], axis=0)
    acc_ref[h] = alpha * acc_ref[h] + _dot(vt_ext, p)[:rows]
    m_ref[h] = m_new


def _silu(z):
    return z * jax.nn.sigmoid(z)


def _causal_steps(n_q_blocks):
    qi = [i for i in range(n_q_blocks) for _ in range(i // SUB + 1)]
    kj = [j for i in range(n_q_blocks) for j in range(i // SUB + 1)]
    return jnp.asarray(qi, I32), jnp.asarray(kj, I32)


def _q_blk(s, qi_ref, kj_ref):
    return qi_ref[s]


def _k_blk(s, qi_ref, kj_ref):
    return kj_ref[s]


def _dsa_kernel(qi_ref, kj_ref, iqT_ref, iwT_ref, ik_ref, qT_ref, k_ref, vT_ref, z_ref, bias_ref, y_ref,
                plane_scr, eq_scr, sel_scr, m_scr, acc_scr, s_scr, *, topk, seq_len):
    i = qi_ref[pl.program_id(0)]
    j = kj_ref[pl.program_id(0)]
    n_bits = max(1, (seq_len - 1).bit_length())
    krow = lax.broadcasted_iota(I32, (KC, TQ), 0)

    @pl.when(j == 0)
    def _select():
        t0 = i * TQ
        w = iwT_ref[...] * (IDX_HEADS ** -0.5 * IDX_DIM ** -0.5)
        qcol = t0 + lax.broadcasted_iota(I32, (KC, TQ), 1)

        def score_chunk(c, causal_mask):
            ks = pl.multiple_of(c * KC, KC)
            kt = ik_ref[pl.ds(ks, KC), :]
            sc = jnp.zeros((KC, TQ), F32)
            for h in range(IDX_HEADS):
                d = _dot(kt, iqT_ref[h * IDX_DIM:(h + 1) * IDX_DIM, :])
                sc = sc + w[h:h + 1, :] * jnp.maximum(d, 0.0)
            sc = jnp.where(sc == 0.0, 0.0, sc)
            key = _order_key(lax.bitcast_convert_type(sc, I32)) ^ INT_MIN
            if causal_mask:
                key = jnp.where(ks + krow <= qcol, key, 0)
            key = key.reshape(WORD_BITS, F32_ROWS, TQ)
            planes = _bit_transpose32([key[r] for r in range(WORD_BITS)])
            for r in range(WORD_BITS):
                plane_scr[c, r] = planes[r]
            eq_scr[c] = jnp.full((F32_ROWS, TQ), -1, I32)
            sel_scr[c] = jnp.zeros((F32_ROWS, TQ), I32)

        def score_body(c, carry):
            score_chunk(c, False)
            return carry

        lax.fori_loop(0, i, score_body, 0)
        score_chunk(i, True)

        n_groups = (i + RADIX_GROUP) // RADIX_GROUP

        def clear_chunk(c, carry):
            plane_scr[c] = jnp.zeros((WORD_BITS, F32_ROWS, TQ), I32)
            eq_scr[c] = jnp.zeros((F32_ROWS, TQ), I32)
            sel_scr[c] = jnp.zeros((F32_ROWS, TQ), I32)
            return carry

        lax.fori_loop(i + 1, n_groups * RADIX_GROUP, clear_chunk, 0)

        def count_bits(word_fn):
            def body(g, acc):
                for u in range(RADIX_GROUP):
                    acc = acc + lax.population_count(word_fn(g * RADIX_GROUP + u))
                return acc
            acc = lax.fori_loop(0, n_groups, body, jnp.zeros((F32_ROWS, TQ), I32))
            return jnp.sum(acc, axis=0, keepdims=True)

        def apply_decision(c, b_prev, drop):
            e = eq_scr[c]
            p = plane_scr[c, b_prev]
            sel_scr[c] = sel_scr[c] | (e & p & drop)
            e = e & (p ^ drop)
            eq_scr[c] = e
            return e

        def decide(cnt_gt, ones):
            take = cnt_gt + ones >= topk
            return jnp.where(take, cnt_gt, cnt_gt + ones), jnp.where(take, 0, -1).astype(I32)

        def radix_step(b, carry):
            cnt_gt, drop = carry
            ones = count_bits(lambda c: apply_decision(c, b - 1, drop) & plane_scr[c, b])
            return decide(cnt_gt, ones)

        first = decide(jnp.zeros((1, TQ), I32), count_bits(lambda c: plane_scr[c, 0]))
        cnt_gt, drop = lax.fori_loop(1, WORD_BITS, radix_step, first)
        cnt_eq = count_bits(lambda c: apply_decision(c, WORD_BITS - 1, drop))

        n_causal = t0 + 1 + lax.broadcasted_iota(I32, (1, TQ), 1)
        full = n_causal >= topk
        need = topk - cnt_gt
        trim = full & (cnt_eq > need)
        jx0 = jnp.where(full, seq_len, 0).astype(I32)

        def tie_search(_):
            def idx_step(b, jx):
                cand = jx | (jnp.int32(1) << (n_bits - 1 - b))
                cnt = count_bits(lambda c: eq_scr[c] & _rows_below(cand - c * KC))
                return jnp.where(cnt < need, cand, jx)
            jx = lax.fori_loop(0, n_bits, idx_step, jnp.zeros((1, TQ), I32))
            return jnp.where(trim, jx + 1, jx0)

        jx = lax.cond(jnp.max(trim.astype(I32)) > 0, tie_search, lambda _: jx0, 0)

        def finalize(c, carry):
            sel_scr[c] = sel_scr[c] | (eq_scr[c] & _rows_below(jx - c * KC))
            return carry

        lax.fori_loop(0, i + 1, finalize, 0)
        m_scr[...] = jnp.full(m_scr.shape, NEG, F32)
        acc_scr[...] = jnp.zeros(acc_scr.shape, F32)

    def consume(buf, c, bidx):
        sel = sel_scr[j * SUB + c]
        addmask = jnp.concatenate(
            [jnp.where((sel << r) < 0, 0.0, NEG) for r in range(WORD_BITS)], axis=0)
        for h in range(A_HEADS):
            lo = h * A_HEAD_DIM
            s = s_scr[buf, h] + addmask
            if bidx is not None:
                s = s + bias_ref[bidx, h]
            _softmax_step(s, m_scr, acc_scr, vT_ref[c, lo:lo + A_HEAD_DIM, :], h)

    d = i - j * SUB

    @pl.when(d > SUB)
    def _far_step():
        _stage_logits(k_ref, qT_ref, s_scr, 0, 0, A_HEADS, A_HEAD_DIM)
        for c in range(SUB):
            if c + 1 < SUB:
                _stage_logits(k_ref, qT_ref, s_scr, (c + 1) % 2, c + 1, A_HEADS, A_HEAD_DIM)
            consume(c % 2, c, None)

    @pl.when((d >= 0) & (d <= SUB))
    def _edge_step():
        def body(c, carry):
            _stage_logits(k_ref, qT_ref, s_scr, 0, c, A_HEADS, A_HEAD_DIM)
            consume(0, c, jnp.minimum(d - c, 2))
            return carry
        lax.fori_loop(0, jnp.minimum(d + 1, SUB), body, 0)

    @pl.when(j == i // SUB)
    def _finish():
        outs = []
        for h in range(A_HEADS):
            acc = acc_scr[h]
            outs.append(acc[:A_HEAD_DIM] * (1.0 / acc[A_HEAD_DIM:A_HEAD_DIM + 1]))
        o = jnp.concatenate(outs, axis=0).T
        y_ref[...] = (o * _silu(z_ref[...])).astype(y_ref.dtype)


def _dsa_mixer(pr, bias_tiles, L):
    topk = min(TOPK_MAX, L // 4)
    qi, kj = _causal_steps(L // TQ)
    grid_spec = pltpu.PrefetchScalarGridSpec(
        num_scalar_prefetch=2,
        grid=(qi.shape[0],),
        in_specs=[
            pl.BlockSpec((IDX_HEADS * IDX_DIM, TQ), lambda *s: (0, _q_blk(*s))),
            pl.BlockSpec((IDX_HEADS, TQ), lambda *s: (0, _q_blk(*s))),
            _resident((L, IDX_DIM)),
            pl.BlockSpec((A_WIDTH, TQ), lambda *s: (0, _q_blk(*s))),
            pl.BlockSpec((TK, A_WIDTH), lambda *s: (_k_blk(*s), 0)),
            pl.BlockSpec((SUB, A_WIDTH, KC), lambda *s: (_k_blk(*s), 0, 0)),
            pl.BlockSpec((TQ, A_WIDTH), lambda *s: (_q_blk(*s), 0)),
            _resident(bias_tiles.shape),
        ],
        out_specs=pl.BlockSpec((TQ, A_WIDTH), lambda *s: (_q_blk(*s), 0)),
        scratch_shapes=[
            pltpu.VMEM((L // KC, WORD_BITS, F32_ROWS, TQ), I32),
            pltpu.VMEM((L // KC, F32_ROWS, TQ), I32),
            pltpu.VMEM((L // KC, F32_ROWS, TQ), I32),
            pltpu.VMEM((A_HEADS, 1, TQ), F32),
            pltpu.VMEM((A_HEADS, A_HEAD_DIM + F32_ROWS, TQ), F32),
            pltpu.VMEM((2, A_HEADS, KC, TQ), F32),
        ],
    )
    return pl.pallas_call(
        functools.partial(_dsa_kernel, topk=topk, seq_len=L),
        grid_spec=grid_spec,
        out_shape=jax.ShapeDtypeStruct((L, A_WIDTH), BF16),
        compiler_params=pltpu.CompilerParams(dimension_semantics=("arbitrary",),
                                             vmem_limit_bytes=VMEM_LIMIT),
        name="dsa",
    )(qi, kj, pr["i_qT"], pr["i_wT"], pr["i_k"], pr["a_qT"], pr["a_k"], pr["a_vT"], pr["a_z"], bias_tiles)


def _diff_kernel(qi_ref, kj_ref, qT_ref, k_ref, vT_ref, z_ref, bias_ref, lam_ref, subw_ref, y_ref,
                 m_scr, acc_scr, s_scr, *, lam_init):
    i = qi_ref[pl.program_id(0)]
    j = kj_ref[pl.program_id(0)]

    @pl.when(j == 0)
    def _init():
        m_scr[...] = jnp.full(m_scr.shape, NEG, F32)
        acc_scr[...] = jnp.zeros(acc_scr.shape, F32)

    def consume(buf, c, edge):
        if edge is not None:
            bidx, addmask = edge
        for mi in range(B_MAPS):
            hb = mi // 2
            s = s_scr[buf, mi]
            if edge is not None:
                s = s + bias_ref[bidx, hb] + addmask
            _softmax_step(s, m_scr, acc_scr, vT_ref[c, hb * B_V_DIM:(hb + 1) * B_V_DIM, :], mi)

    d = i - j * SUB

    @pl.when(d > SUB)
    def _far_step():
        _stage_logits(k_ref, qT_ref, s_scr, 0, 0, B_MAPS, B_QK_DIM)
        for c in range(SUB):
            if c + 1 < SUB:
                _stage_logits(k_ref, qT_ref, s_scr, (c + 1) % 2, c + 1, B_MAPS, B_QK_DIM)
            consume(c % 2, c, None)

    @pl.when((d >= 0) & (d <= SUB))
    def _edge_step():
        krow = lax.broadcasted_iota(I32, (KC, TQ), 0)
        qcol = lax.broadcasted_iota(I32, (KC, TQ), 1)

        def body(c, carry):
            _stage_logits(k_ref, qT_ref, s_scr, 0, c, B_MAPS, B_QK_DIM)
            addmask = jnp.where(krow <= (d - c) * KC + qcol, 0.0, NEG)
            consume(0, c, (jnp.minimum(d - c, 2), addmask))
            return carry
        lax.fori_loop(0, jnp.minimum(d + 1, SUB), body, 0)

    @pl.when(j == i // SUB)
    def _finish():
        lq = lam_ref[...]
        lam = (jnp.exp(jnp.sum(lq[0:1, :] * lq[1:2, :], axis=1, keepdims=True))
               - jnp.exp(jnp.sum(lq[2:3, :] * lq[3:4, :], axis=1, keepdims=True)) + lam_init)
        outs = []
        for hb in range(B_HEADS):
            a1 = acc_scr[2 * hb]
            a2 = acc_scr[2 * hb + 1]
            o = (a1[:B_V_DIM] * (1.0 / a1[B_V_DIM:B_V_DIM + 1])
                 - lam * (a2[:B_V_DIM] * (1.0 / a2[B_V_DIM:B_V_DIM + 1])))
            ms = jnp.mean(o * o, axis=0, keepdims=True)
            outs.append(o * lax.rsqrt(ms + RMS_EPS) * subw_ref[...] * (1.0 - lam_init))
        o = jnp.concatenate(outs, axis=0).T
        y_ref[...] = (o * _silu(z_ref[...])).astype(y_ref.dtype)


def _diff_mixer(pr, bias_tiles, lambda_qk, subln_w, lam_init, L):
    qi, kj = _causal_steps(L // TQ)
    grid_spec = pltpu.PrefetchScalarGridSpec(
        num_scalar_prefetch=2,
        grid=(qi.shape[0],),
        in_specs=[
            pl.BlockSpec((B_WIDTH, TQ), lambda *s: (0, _q_blk(*s))),
            pl.BlockSpec((TK, B_WIDTH), lambda *s: (_k_blk(*s), 0)),
            pl.BlockSpec((SUB, B_WIDTH, KC), lambda *s: (_k_blk(*s), 0, 0)),
            pl.BlockSpec((TQ, B_WIDTH), lambda *s: (_q_blk(*s), 0)),
            _resident(bias_tiles.shape),
            _resident(lambda_qk.shape),
            _resident(subln_w.shape),
        ],
        out_specs=pl.BlockSpec((TQ, B_WIDTH), lambda *s: (_q_blk(*s), 0)),
        scratch_shapes=[
            pltpu.VMEM((B_MAPS, 1, TQ), F32),
            pltpu.VMEM((B_MAPS, B_V_DIM + F32_ROWS, TQ), F32),
            pltpu.VMEM((2, B_MAPS, KC, TQ), F32),
        ],
    )
    return pl.pallas_call(
        functools.partial(_diff_kernel, lam_init=lam_init),
        grid_spec=grid_spec,
        out_shape=jax.ShapeDtypeStruct((L, B_WIDTH), BF16),
        compiler_params=pltpu.CompilerParams(dimension_semantics=("arbitrary",),
                                             vmem_limit_bytes=VMEM_LIMIT),
        name="diff",
    )(qi, kj, pr["b_qT"], pr["b_k"], pr["b_vT"], pr["b_z"], bias_tiles, lambda_qk, subln_w)


def _merge_kernel(x_ref, p_ref, ya_ref, yb_ref, ga_ref, gb_ref, wpa_ref, wpb_ref, wo_ref,
                  lng_ref, lnb_ref, wple_ref, wgate_ref, out_ref):
    merged = (jax.nn.sigmoid(ga_ref[...]) * _dot(ya_ref[...], wpa_ref[...])
              + jax.nn.sigmoid(gb_ref[...]) * _dot(yb_ref[...], wpb_ref[...]))
    mix = _dot(merged.astype(BF16), wo_ref[...])
    z = DEEPNORM_ALPHA * x_ref[...] + mix
    mu = jnp.mean(z, axis=-1, keepdims=True)
    zc = z - mu
    var = jnp.mean(zc * zc, axis=-1, keepdims=True)
    x1 = zc * lax.rsqrt(var + LN_EPS) * lng_ref[...] + lnb_ref[...]
    gate = jax.nn.sigmoid(_dot(x1.astype(BF16), wgate_ref[...]))
    out_ref[...] = x1 + gate * _dot(p_ref[...].astype(BF16), wple_ref[...])


def _merge(x2, p2, y_a, y_b, g_a, g_b, w_pa, w_pb, w_o, ln_g, ln_b, w_ple, w_gate):
    L = x2.shape[0]
    rows = MERGE_ROWS
    row_blk = lambda width: pl.BlockSpec((rows, width), lambda i: (i, 0))
    weights = (w_pa.astype(BF16), w_pb.astype(BF16), w_o.astype(BF16),
               ln_g.reshape(1, D_MODEL), ln_b.reshape(1, D_MODEL), w_ple.astype(BF16), w_gate.astype(BF16))
    return pl.pallas_call(
        _merge_kernel,
        grid=(L // rows,),
        in_specs=[row_blk(D_MODEL), row_blk(PLE_DIM), row_blk(A_WIDTH), row_blk(B_WIDTH),
                  row_blk(D_MODEL), row_blk(D_MODEL)] + [_resident(a.shape) for a in weights],
        out_specs=row_blk(D_MODEL),
        out_shape=jax.ShapeDtypeStruct((L, D_MODEL), F32),
        compiler_params=pltpu.CompilerParams(dimension_semantics=("arbitrary",),
                                             vmem_limit_bytes=VMEM_LIMIT),
        name="merge",
    )(x2, p2, y_a, y_b, g_a, g_b, *weights)


def kernel(x, p, w_in, w_pa, w_pb, w_o, lambda_qk, subln_w, ln_g, ln_b, w_ple, w_ple_gate, rel_bias):
    b, L, _ = x.shape
    assert b == 1 and w_in.shape[0] == DEPTH == 1
    assert L % TK == 0 and TQ == KC and KC >= MAX_DISTANCE and KC == WORD_BITS * F32_ROWS
    tiles = _bias_tiles(rel_bias)
    x2 = x[0]
    pr = _project(x2, w_in[0])
    y_a = _dsa_mixer(pr, tiles[:, :A_HEADS], L)
    lam_init = 0.8 - 0.6 * math.exp(-0.3 * 0)
    y_b = _diff_mixer(pr, tiles[:, A_HEADS:], lambda_qk[0], subln_w[0].reshape(B_V_DIM, 1), lam_init, L)
    out = _merge(x2, p[0, 0], y_a, y_b, pr["g_a"], pr["g_b"], w_pa[0], w_pb[0], w_o[0],
                 ln_g[0], ln_b[0], w_ple[0], w_ple_gate[0])
    return out[None]
```

```python
import functools
import math

import jax
import jax.numpy as jnp
from jax import lax
from jax.experimental import pallas as pl
from jax.experimental.pallas import tpu as pltpu

D_MODEL = 1024
A_HEADS = 8
A_HEAD_DIM = 64
A_WIDTH = A_HEADS * A_HEAD_DIM
IDX_HEADS = 8
IDX_DIM = 64
TOPK_MAX = 256
B_HEADS = 4
B_QK_DIM = 64
B_MAPS = 2 * B_HEADS
B_V_DIM = 2 * B_QK_DIM
B_WIDTH = B_HEADS * B_V_DIM
N_BUCKETS = 32
MAX_DISTANCE = 128
PLE_DIM = 256
LN_EPS = 1e-5
RMS_EPS = 1e-5
DEPTH = 1
DEEPNORM_ALPHA = (2 * DEPTH) ** 0.25
LOG2E = math.log2(math.e)

IN_SIZES = (
    A_WIDTH, A_WIDTH, A_WIDTH, A_WIDTH,
    IDX_HEADS * IDX_DIM, IDX_DIM, IDX_HEADS,
    B_HEADS * 2 * B_QK_DIM, B_HEADS * 2 * B_QK_DIM,
    B_WIDTH, B_WIDTH,
    D_MODEL, D_MODEL,
)

BF16 = jnp.bfloat16
F32 = jnp.float32
I32 = jnp.int32

TQ = 256
KC = 256
SUB = 8
TK = SUB * KC
PROJ_ROWS = KC
MERGE_ROWS = 256
VMEM_LIMIT = 56 * 1024 * 1024
BF16_ROWS = 16
F32_ROWS = 8
WORD_BITS = 32
RADIX_GROUP = 4

NEG = -1e30
INT_MIN = -(2 ** 31)


def _nt_dot(a, b):
    return lax.dot_general(a, b, (((1,), (1,)), ((), ())), preferred_element_type=F32)


def _dot(a, b):
    return jnp.dot(a, b, preferred_element_type=F32)


def _order_key(bits):
    return bits ^ ((bits >> 31) & 0x7FFFFFFF)


def _bit_transpose32(words):
    w = list(words)
    shift, mask = 16, 0x0000FFFF
    while shift:
        for k in range(WORD_BITS):
            if not k & shift:
                t = (w[k] ^ (w[k + shift] >> shift)) & mask
                w[k] = w[k] ^ t
                w[k + shift] = w[k + shift] ^ (t << shift)
        shift >>= 1
        mask = (mask ^ (mask << shift)) & 0xFFFFFFFF
    return w


def _rows_below(limit):
    sub = lax.broadcasted_iota(I32, (F32_ROWS, TQ), 0)
    n = jnp.clip((limit - sub + (F32_ROWS - 1)) >> 3, 0, WORD_BITS)
    return jnp.where(n > 0, jnp.int32(INT_MIN) >> (jnp.maximum(n, 1) - 1), 0)


def _resident(shape):
    return pl.BlockSpec(shape, lambda *_: (0,) * len(shape), pipeline_mode=pl.Buffered(1))


_N_OUTS = (("a_k", A_WIDTH, BF16), ("b_k", B_WIDTH, BF16), ("a_z", A_WIDTH, F32), ("b_z", B_WIDTH, F32),
           ("g_a", D_MODEL, F32), ("g_b", D_MODEL, F32), ("i_k", IDX_DIM, BF16))
_T_OUTS = (("a_qT", A_WIDTH, BF16, A_HEAD_DIM ** -0.5 * LOG2E, False), ("a_vT", A_WIDTH, BF16, 1.0, True),
           ("i_qT", IDX_HEADS * IDX_DIM, BF16, 1.0, False), ("b_qT", B_WIDTH, BF16, B_QK_DIM ** -0.5 * LOG2E, False),
           ("b_vT", B_WIDTH, BF16, 1.0, True), ("i_wT", IDX_HEADS, F32, 1.0, False))


def _proj_kernel(x_ref, wn_ref, wt_ref, *out_refs):
    xb = x_ref[...].astype(BF16)
    off = 0
    for k, (_, width, dtype) in enumerate(_N_OUTS):
        out_refs[k][...] = _dot(xb, wn_ref[:, off:off + width]).astype(dtype)
        off += width
    off = 0
    for k, (_, width, dtype, scale, chunked) in enumerate(_T_OUTS):
        r = _nt_dot(wt_ref[off:off + width, :], xb)
        if scale != 1.0:
            r = r * scale
        o_ref = out_refs[len(_N_OUTS) + k]
        if chunked:
            o_ref[0] = r.astype(dtype)
        else:
            o_ref[...] = r.astype(dtype)
        off += width


def _project(x2, w):
    L = x2.shape[0]
    pts = [0]
    for s in IN_SIZES:
        pts.append(pts[-1] + s)
    names = ("a_q", "a_k", "a_v", "a_z", "i_q", "i_k", "i_w", "b_q", "b_k", "b_v", "b_z", "g_a", "g_b")
    col = {n: w[:, pts[k]:pts[k + 1]] for k, n in enumerate(names)}
    wn = jnp.concatenate([col[n] for n, _, _ in _N_OUTS], axis=1).astype(BF16)
    wt = jnp.concatenate([col[n[:-1]] for n, _, _, _, _ in _T_OUTS], axis=1).T.astype(BF16)
    rows = PROJ_ROWS
    out_shape, out_specs = [], []
    for _, width, dtype in _N_OUTS:
        out_shape.append(jax.ShapeDtypeStruct((L, width), dtype))
        out_specs.append(pl.BlockSpec((rows, width), lambda i: (i, 0)))
    for _, width, dtype, _, chunked in _T_OUTS:
        if chunked:
            out_shape.append(jax.ShapeDtypeStruct((L // rows, width, rows), dtype))
            out_specs.append(pl.BlockSpec((1, width, rows), lambda i: (i, 0, 0)))
        else:
            out_shape.append(jax.ShapeDtypeStruct((width, L), dtype))
            out_specs.append(pl.BlockSpec((width, rows), lambda i: (0, i)))
    outs = pl.pallas_call(
        _proj_kernel,
        grid=(L // rows,),
        in_specs=[pl.BlockSpec((rows, D_MODEL), lambda i: (i, 0)), _resident(wn.shape), _resident(wt.shape)],
        out_specs=out_specs,
        out_shape=out_shape,
        compiler_params=pltpu.CompilerParams(dimension_semantics=("arbitrary",),
                                             vmem_limit_bytes=VMEM_LIMIT),
        name="proj",
    )(x2, wn, wt)
    res = {n: o for (n, _, _), o in zip(_N_OUTS, outs[:len(_N_OUTS)])}
    res.update({n: o for (n, _, _, _, _), o in zip(_T_OUTS, outs[len(_N_OUTS):])})
    return res


def _rel_bucket(dist):
    n = jnp.maximum(dist, 0)
    max_exact = N_BUCKETS // 2
    nf = jnp.maximum(n, 1).astype(F32)
    large = max_exact + (jnp.log(nf / max_exact) / math.log(MAX_DISTANCE / max_exact)
                         * (N_BUCKETS - max_exact)).astype(I32)
    large = jnp.minimum(large, N_BUCKETS - 1)
    return jnp.where(n < max_exact, n, large)


def _bias_tiles(rel_bias):
    kk = jnp.arange(KC)[:, None]
    qq = jnp.arange(TQ)[None, :]
    shifted = rel_bias - rel_bias[N_BUCKETS - 1]
    tiles = []
    for d in range(2):
        onehot = jax.nn.one_hot(_rel_bucket(d * KC + qq - kk), N_BUCKETS, dtype=F32)
        tiles.append(jnp.einsum("kqb,bh->hkq", onehot, shifted, precision=lax.Precision.HIGHEST))
    tiles.append(jnp.zeros_like(tiles[0]))
    return (jnp.stack(tiles) * LOG2E).astype(F32)


def _stage_logits(k_ref, qT_ref, s_scr, buf, c, n_streams, dim):
    ks = c * KC if isinstance(c, int) else pl.multiple_of(c * KC, KC)
    for h in range(n_streams):
        lo = h * dim
        s_scr[buf, h] = _dot(k_ref[pl.ds(ks, KC), lo:lo + dim], qT_ref[lo:lo + dim, :])


def _softmax_step(s, m_ref, acc_ref, vt, h):
    rows = acc_ref.shape[1]
    m_old = m_ref[h]
    m_new = jnp.maximum(m_old, jnp.max(s, axis=0, keepdims=True))
    alpha = jnp.exp2(m_old - m_new)
    p = jnp.exp2(s - m_new).astype(BF16)
    vt_ext = jnp.concatenate([vt, jnp.ones((BF16_ROWS, KC), BF16)], axis=0)
    acc_ref[h] = alpha * acc_ref[h] + _dot(vt_ext, p)[:rows]
    m_ref[h] = m_new


def _attend_step(d, stage, consume_far, consume_edge):
    @pl.when(d > SUB)
    def _far_step():
        stage(0, 0)
        for c in range(SUB):
            if c + 1 < SUB:
                stage((c + 1) % 2, c + 1)
            consume_far(c % 2, c)

    @pl.when((d >= 0) & (d <= SUB))
    def _edge_step():
        def body(c, carry):
            stage(0, c)
            consume_edge(0, c)
            return carry

        lax.fori_loop(0, jnp.minimum(d + 1, SUB), body, 0)


def _silu(z):
    return z * jax.nn.sigmoid(z)


def _causal_steps(n_q_blocks):
    qi = [i for i in range(n_q_blocks) for _ in range(i // SUB + 1)]
    kj = [j for i in range(n_q_blocks) for j in range(i // SUB + 1)]
    return jnp.asarray(qi, I32), jnp.asarray(kj, I32)


def _q_blk(s, qi_ref, kj_ref):
    return qi_ref[s]


def _k_blk(s, qi_ref, kj_ref):
    return kj_ref[s]


def _score_chunk(ik_ref, iqT_ref, w, kc, causal_limit, plane_scr, eq_scr, sel_scr):
    kt = ik_ref[pl.ds(pl.multiple_of(kc * KC, KC), KC), :]
    sc = jnp.zeros((KC, TQ), F32)
    for h in range(IDX_HEADS):
        dots = _dot(kt, iqT_ref[h * IDX_DIM:(h + 1) * IDX_DIM, :])
        sc = sc + w[h:h + 1, :] * jnp.maximum(dots, 0.0)
    sc = jnp.where(sc == 0.0, 0.0, sc)
    key = _order_key(lax.bitcast_convert_type(sc, I32)) ^ INT_MIN
    if causal_limit is not None:
        krow = lax.broadcasted_iota(I32, (KC, TQ), 0)
        qcol = lax.broadcasted_iota(I32, (KC, TQ), 1)
        key = jnp.where(krow <= causal_limit + qcol, key, 0)
    key = key.reshape(WORD_BITS, F32_ROWS, TQ)
    planes = _bit_transpose32([key[r] for r in range(WORD_BITS)])
    for r in range(WORD_BITS):
        plane_scr[kc, r] = planes[r]
    eq_scr[kc] = jnp.full((F32_ROWS, TQ), -1, I32)
    sel_scr[kc] = jnp.zeros((F32_ROWS, TQ), I32)


def _radix_select(i, plane_scr, eq_scr, sel_scr, sel_ref, topk, seq_len):
    n_chunks = sel_ref.shape[1]
    n_bits = max(1, (seq_len - 1).bit_length())
    n_groups = (i + RADIX_GROUP) // RADIX_GROUP

    def clear_chunk(c, carry):
        plane_scr[c] = jnp.zeros((WORD_BITS, F32_ROWS, TQ), I32)
        eq_scr[c] = jnp.zeros((F32_ROWS, TQ), I32)
        sel_scr[c] = jnp.zeros((F32_ROWS, TQ), I32)
        return carry

    lax.fori_loop(i + 1, n_groups * RADIX_GROUP, clear_chunk, 0)

    def count_bits(word_fn):
        def body(g, acc):
            for u in range(RADIX_GROUP):
                acc = acc + lax.population_count(word_fn(g * RADIX_GROUP + u))
            return acc
        acc = lax.fori_loop(0, n_groups, body, jnp.zeros((F32_ROWS, TQ), I32))
        return jnp.sum(acc, axis=0, keepdims=True)

    def apply_decision(c, b_prev, drop):
        e = eq_scr[c]
        p = plane_scr[c, b_prev]
        sel_scr[c] = sel_scr[c] | (e & p & drop)
        e = e & (p ^ drop)
        eq_scr[c] = e
        return e

    def decide(cnt_gt, ones):
        take = cnt_gt + ones >= topk
        return jnp.where(take, cnt_gt, cnt_gt + ones), jnp.where(take, 0, -1).astype(I32)

    def radix_step(b, carry):
        cnt_gt, drop = carry
        ones = count_bits(lambda c: apply_decision(c, b - 1, drop) & plane_scr[c, b])
        return decide(cnt_gt, ones)

    first = decide(jnp.zeros((1, TQ), I32), count_bits(lambda c: plane_scr[c, 0]))
    cnt_gt, drop = lax.fori_loop(1, WORD_BITS, radix_step, first)
    cnt_eq = count_bits(lambda c: apply_decision(c, WORD_BITS - 1, drop))

    n_causal = i * TQ + 1 + lax.broadcasted_iota(I32, (1, TQ), 1)
    full = n_causal >= topk
    need = topk - cnt_gt
    trim = full & (cnt_eq > need)
    jx0 = jnp.where(full, seq_len, 0).astype(I32)

    def tie_search(_):
        def idx_step(b, jx):
            cand = jx | (jnp.int32(1) << (n_bits - 1 - b))
            cnt = count_bits(lambda c: eq_scr[c] & _rows_below(cand - c * KC))
            return jnp.where(cnt < need, cand, jx)
        jx = lax.fori_loop(0, n_bits, idx_step, jnp.zeros((1, TQ), I32))
        return jnp.where(trim, jx + 1, jx0)

    jx = lax.cond(jnp.max(trim.astype(I32)) > 0, tie_search, lambda _: jx0, 0)

    def emit(c, carry):
        sel_ref[0, c] = sel_scr[c] | (eq_scr[c] & _rows_below(jx - c * KC))
        return carry

    lax.fori_loop(0, i + 1, emit, 0)

    def emit_empty(c, carry):
        sel_ref[0, c] = jnp.zeros((F32_ROWS, TQ), I32)
        return carry

    lax.fori_loop(i + 1, n_chunks, emit_empty, 0)


def _diff_kernel(qi_ref, kj_ref, qT_ref, k_ref, vT_ref, z_ref, bias_ref, lam_ref, subw_ref,
                 iqT_ref, iwT_ref, ik_ref, y_ref, sel_ref,
                 m_scr, acc_scr, s_scr, plane_scr, eq_scr, sel_scr, *, lam_init, topk, seq_len):
    i = qi_ref[pl.program_id(0)]
    j = kj_ref[pl.program_id(0)]
    d = i - j * SUB

    @pl.when(j == 0)
    def _init():
        m_scr[...] = jnp.full(m_scr.shape, NEG, F32)
        acc_scr[...] = jnp.zeros(acc_scr.shape, F32)

    w = iwT_ref[...] * (IDX_HEADS ** -0.5 * IDX_DIM ** -0.5)

    def stage(buf, c):
        _stage_logits(k_ref, qT_ref, s_scr, buf, c, B_MAPS, B_QK_DIM)

    def consume(buf, c, edge):
        if edge:
            krow = lax.broadcasted_iota(I32, (KC, TQ), 0)
            qcol = lax.broadcasted_iota(I32, (KC, TQ), 1)
            addmask = jnp.where(krow <= (d - c) * KC + qcol, 0.0, NEG)
            bidx = jnp.minimum(d - c, 2)
        for mi in range(B_MAPS):
            hb = mi // 2
            s = s_scr[buf, mi]
            if edge:
                s = s + bias_ref[bidx, hb] + addmask
            _softmax_step(s, m_scr, acc_scr, vT_ref[c, hb * B_V_DIM:(hb + 1) * B_V_DIM, :], mi)
        _score_chunk(ik_ref, iqT_ref, w, j * SUB + c, (d - c) * KC if edge else None,
                     plane_scr, eq_scr, sel_scr)

    _attend_step(d, stage, functools.partial(consume, edge=False), functools.partial(consume, edge=True))

    @pl.when(j == i // SUB)
    def _finish():
        lq = lam_ref[...]
        lam = (jnp.exp(jnp.sum(lq[0:1, :] * lq[1:2, :], axis=1, keepdims=True))
               - jnp.exp(jnp.sum(lq[2:3, :] * lq[3:4, :], axis=1, keepdims=True)) + lam_init)
        outs = []
        for hb in range(B_HEADS):
            a1 = acc_scr[2 * hb]
            a2 = acc_scr[2 * hb + 1]
            o = (a1[:B_V_DIM] * (1.0 / a1[B_V_DIM:B_V_DIM + 1])
                 - lam * (a2[:B_V_DIM] * (1.0 / a2[B_V_DIM:B_V_DIM + 1])))
            ms = jnp.mean(o * o, axis=0, keepdims=True)
            outs.append(o * lax.rsqrt(ms + RMS_EPS) * subw_ref[...] * (1.0 - lam_init))
        o = jnp.concatenate(outs, axis=0).T
        y_ref[...] = (o * _silu(z_ref[...])).astype(y_ref.dtype)
        _radix_select(i, plane_scr, eq_scr, sel_scr, sel_ref, topk, seq_len)


def _diff_mixer(pr, bias_tiles, lambda_qk, subln_w, lam_init, L):
    topk = min(TOPK_MAX, L // 4)
    n_chunks = L // KC
    qi, kj = _causal_steps(L // TQ)
    grid_spec = pltpu.PrefetchScalarGridSpec(
        num_scalar_prefetch=2,
        grid=(qi.shape[0],),
        in_specs=[
            pl.BlockSpec((B_WIDTH, TQ), lambda *s: (0, _q_blk(*s))),
            pl.BlockSpec((TK, B_WIDTH), lambda *s: (_k_blk(*s), 0)),
            pl.BlockSpec((SUB, B_WIDTH, KC), lambda *s: (_k_blk(*s), 0, 0)),
            pl.BlockSpec((TQ, B_WIDTH), lambda *s: (_q_blk(*s), 0)),
            _resident(bias_tiles.shape),
            _resident(lambda_qk.shape),
            _resident(subln_w.shape),
            pl.BlockSpec((IDX_HEADS * IDX_DIM, TQ), lambda *s: (0, _q_blk(*s))),
            pl.BlockSpec((IDX_HEADS, TQ), lambda *s: (0, _q_blk(*s))),
            _resident((L, IDX_DIM)),
        ],
        out_specs=[
            pl.BlockSpec((TQ, B_WIDTH), lambda *s: (_q_blk(*s), 0)),
            pl.BlockSpec((1, n_chunks, F32_ROWS, TQ), lambda *s: (_q_blk(*s), 0, 0, 0)),
        ],
        scratch_shapes=[
            pltpu.VMEM((B_MAPS, 1, TQ), F32),
            pltpu.VMEM((B_MAPS, B_V_DIM + F32_ROWS, TQ), F32),
            pltpu.VMEM((2, B_MAPS, KC, TQ), F32),
            pltpu.VMEM((n_chunks, WORD_BITS, F32_ROWS, TQ), I32),
            pltpu.VMEM((n_chunks, F32_ROWS, TQ), I32),
            pltpu.VMEM((n_chunks, F32_ROWS, TQ), I32),
        ],
    )
    return pl.pallas_call(
        functools.partial(_diff_kernel, lam_init=lam_init, topk=topk, seq_len=L),
        grid_spec=grid_spec,
        out_shape=[jax.ShapeDtypeStruct((L, B_WIDTH), BF16),
                   jax.ShapeDtypeStruct((L // TQ, n_chunks, F32_ROWS, TQ), I32)],
        compiler_params=pltpu.CompilerParams(dimension_semantics=("arbitrary",),
                                             vmem_limit_bytes=VMEM_LIMIT),
        name="diff",
    )(qi, kj, pr["b_qT"], pr["b_k"], pr["b_vT"], pr["b_z"], bias_tiles, lambda_qk, subln_w,
      pr["i_qT"], pr["i_wT"], pr["i_k"])


def _dsa_kernel(qi_ref, kj_ref, qT_ref, k_ref, vT_ref, z_ref, bias_ref, sel_ref, y_ref,
                m_scr, acc_scr, s_scr):
    i = qi_ref[pl.program_id(0)]
    j = kj_ref[pl.program_id(0)]
    d = i - j * SUB

    @pl.when(j == 0)
    def _init():
        m_scr[...] = jnp.full(m_scr.shape, NEG, F32)
        acc_scr[...] = jnp.zeros(acc_scr.shape, F32)

    def stage(buf, c):
        _stage_logits(k_ref, qT_ref, s_scr, buf, c, A_HEADS, A_HEAD_DIM)

    def consume(buf, c, edge):
        sel = sel_ref[0, c]
        addmask = jnp.concatenate(
            [jnp.where((sel << r) < 0, 0.0, NEG) for r in range(WORD_BITS)], axis=0)
        for h in range(A_HEADS):
            lo = h * A_HEAD_DIM
            s = s_scr[buf, h] + addmask
            if edge:
                s = s + bias_ref[jnp.minimum(d - c, 2), h]
            _softmax_step(s, m_scr, acc_scr, vT_ref[c, lo:lo + A_HEAD_DIM, :], h)

    _attend_step(d, stage, functools.partial(consume, edge=False), functools.partial(consume, edge=True))

    @pl.when(j == i // SUB)
    def _finish():
        outs = []
        for h in range(A_HEADS):
            acc = acc_scr[h]
            outs.append(acc[:A_HEAD_DIM] * (1.0 / acc[A_HEAD_DIM:A_HEAD_DIM + 1]))
        o = jnp.concatenate(outs, axis=0).T
        y_ref[...] = (o * _silu(z_ref[...])).astype(y_ref.dtype)


def _dsa_mixer(pr, bias_tiles, sel_bits, L):
    qi, kj = _causal_steps(L // TQ)
    grid_spec = pltpu.PrefetchScalarGridSpec(
        num_scalar_prefetch=2,
        grid=(qi.shape[0],),
        in_specs=[
            pl.BlockSpec((A_WIDTH, TQ), lambda *s: (0, _q_blk(*s))),
            pl.BlockSpec((TK, A_WIDTH), lambda *s: (_k_blk(*s), 0)),
            pl.BlockSpec((SUB, A_WIDTH, KC), lambda *s: (_k_blk(*s), 0, 0)),
            pl.BlockSpec((TQ, A_WIDTH), lambda *s: (_q_blk(*s), 0)),
            _resident(bias_tiles.shape),
            pl.BlockSpec((1, SUB, F32_ROWS, TQ), lambda *s: (_q_blk(*s), _k_blk(*s), 0, 0)),
        ],
        out_specs=pl.BlockSpec((TQ, A_WIDTH), lambda *s: (_q_blk(*s), 0)),
        scratch_shapes=[
            pltpu.VMEM((A_HEADS, 1, TQ), F32),
            pltpu.VMEM((A_HEADS, A_HEAD_DIM + F32_ROWS, TQ), F32),
            pltpu.VMEM((2, A_HEADS, KC, TQ), F32),
        ],
    )
    return pl.pallas_call(
        _dsa_kernel,
        grid_spec=grid_spec,
        out_shape=jax.ShapeDtypeStruct((L, A_WIDTH), BF16),
        compiler_params=pltpu.CompilerParams(dimension_semantics=("arbitrary",),
                                             vmem_limit_bytes=VMEM_LIMIT),
        name="dsa",
    )(qi, kj, pr["a_qT"], pr["a_k"], pr["a_vT"], pr["a_z"], bias_tiles, sel_bits)


def _merge_kernel(x_ref, p_ref, ya_ref, yb_ref, ga_ref, gb_ref, wpa_ref, wpb_ref, wo_ref,
                  lng_ref, lnb_ref, wple_ref, wgate_ref, out_ref):
    merged = (jax.nn.sigmoid(ga_ref[...]) * _dot(ya_ref[...], wpa_ref[...])
              + jax.nn.sigmoid(gb_ref[...]) * _dot(yb_ref[...], wpb_ref[...]))
    mix = _dot(merged.astype(BF16), wo_ref[...])
    z = DEEPNORM_ALPHA * x_ref[...] + mix
    mu = jnp.mean(z, axis=-1, keepdims=True)
    zc = z - mu
    var = jnp.mean(zc * zc, axis=-1, keepdims=True)
    x1 = zc * lax.rsqrt(var + LN_EPS) * lng_ref[...] + lnb_ref[...]
    gate = jax.nn.sigmoid(_dot(x1.astype(BF16), wgate_ref[...]))
    out_ref[...] = x1 + gate * _dot(p_ref[...].astype(BF16), wple_ref[...])


def _merge(x2, p2, y_a, y_b, g_a, g_b, w_pa, w_pb, w_o, ln_g, ln_b, w_ple, w_gate):
    L = x2.shape[0]
    rows = MERGE_ROWS
    row_blk = lambda width: pl.BlockSpec((rows, width), lambda i: (i, 0))
    weights = (w_pa.astype(BF16), w_pb.astype(BF16), w_o.astype(BF16),
               ln_g.reshape(1, D_MODEL), ln_b.reshape(1, D_MODEL), w_ple.astype(BF16), w_gate.astype(BF16))
    return pl.pallas_call(
        _merge_kernel,
        grid=(L // rows,),
        in_specs=[row_blk(D_MODEL), row_blk(PLE_DIM), row_blk(A_WIDTH), row_blk(B_WIDTH),
                  row_blk(D_MODEL), row_blk(D_MODEL)] + [_resident(a.shape) for a in weights],
        out_specs=row_blk(D_MODEL),
        out_shape=jax.ShapeDtypeStruct((L, D_MODEL), F32),
        compiler_params=pltpu.CompilerParams(dimension_semantics=("arbitrary",),
                                             vmem_limit_bytes=VMEM_LIMIT),
        name="merge",
    )(x2, p2, y_a, y_b, g_a, g_b, *weights)


def kernel(x, p, w_in, w_pa, w_pb, w_o, lambda_qk, subln_w, ln_g, ln_b, w_ple, w_ple_gate, rel_bias):
    b, L, _ = x.shape
    assert b == 1 and w_in.shape[0] == DEPTH == 1
    assert L % TK == 0 and TQ == KC and KC >= MAX_DISTANCE and KC == WORD_BITS * F32_ROWS
    tiles = _bias_tiles(rel_bias)
    x2 = x[0]
    pr = _project(x2, w_in[0])
    lam_init = 0.8 - 0.6 * math.exp(-0.3 * 0)
    y_b, sel_bits = _diff_mixer(pr, tiles[:, A_HEADS:], lambda_qk[0], subln_w[0].reshape(B_V_DIM, 1),
                                lam_init, L)
    y_a = _dsa_mixer(pr, tiles[:, :A_HEADS], sel_bits, L)
    out = _merge(x2, p[0, 0], y_a, y_b, pr["g_a"], pr["g_b"], w_pa[0], w_pb[0], w_o[0],
                 ln_g[0], ln_b[0], w_ple[0], w_ple_gate[0])
    return out[None]
```

```python
import functools
import math

import jax
import jax.numpy as jnp
from jax import lax
from jax.experimental import pallas as pl
from jax.experimental.pallas import tpu as pltpu

D_MODEL = 1024
A_HEADS = 8
A_HEAD_DIM = 64
A_WIDTH = A_HEADS * A_HEAD_DIM
IDX_HEADS = 8
IDX_DIM = 64
TOPK_MAX = 256
B_HEADS = 4
B_QK_DIM = 64
B_MAPS = 2 * B_HEADS
B_V_DIM = 2 * B_QK_DIM
B_WIDTH = B_HEADS * B_V_DIM
N_BUCKETS = 32
MAX_DISTANCE = 128
PLE_DIM = 256
LN_EPS = 1e-5
RMS_EPS = 1e-5
DEPTH = 1
DEEPNORM_ALPHA = (2 * DEPTH) ** 0.25
LOG2E = math.log2(math.e)

IN_SIZES = (
    A_WIDTH, A_WIDTH, A_WIDTH, A_WIDTH,
    IDX_HEADS * IDX_DIM, IDX_DIM, IDX_HEADS,
    B_HEADS * 2 * B_QK_DIM, B_HEADS * 2 * B_QK_DIM,
    B_WIDTH, B_WIDTH,
    D_MODEL, D_MODEL,
)

BF16 = jnp.bfloat16
F32 = jnp.float32
I32 = jnp.int32

TQ = 256
KC = 256
SUB = 8
TK = SUB * KC
PROJ_ROWS = KC
MERGE_ROWS = 256
VMEM_LIMIT = 56 * 1024 * 1024
BF16_ROWS = 16
F32_ROWS = 8
WORD_BITS = 32
RADIX_GROUP = 8

NEG = -1e30
INT_MIN = -(2 ** 31)


def _nt_dot(a, b):
    return lax.dot_general(a, b, (((1,), (1,)), ((), ())), preferred_element_type=F32)


def _dot(a, b):
    return jnp.dot(a, b, preferred_element_type=F32)


def _order_key(bits):
    return bits ^ ((bits >> 31) | INT_MIN)


def _bit_transpose32(words):
    w = list(words)
    shift, mask = 16, 0x0000FFFF
    while shift:
        for k in range(WORD_BITS):
            if not k & shift:
                t = (w[k] ^ (w[k + shift] >> shift)) & mask
                w[k] = w[k] ^ t
                w[k + shift] = w[k + shift] ^ (t << shift)
        shift >>= 1
        mask = (mask ^ (mask << shift)) & 0xFFFFFFFF
    return w


def _rows_below(limit):
    sub = lax.broadcasted_iota(I32, (F32_ROWS, TQ), 0)
    n = jnp.clip((limit - sub + (F32_ROWS - 1)) >> 3, 0, WORD_BITS)
    return jnp.where(n > 0, jnp.int32(INT_MIN) >> (jnp.maximum(n, 1) - 1), 0)


def _resident(shape):
    return pl.BlockSpec(shape, lambda *_: (0,) * len(shape), pipeline_mode=pl.Buffered(1))


_OUTS = (
    ("a_k", A_WIDTH, BF16, BF16, False, 1.0, False),
    ("b_k", B_WIDTH, BF16, BF16, False, 1.0, False),
    ("a_z", A_WIDTH, F32, BF16, False, 1.0, False),
    ("b_z", B_WIDTH, F32, BF16, False, 1.0, False),
    ("g_a", D_MODEL, F32, BF16, False, 1.0, False),
    ("g_b", D_MODEL, F32, BF16, False, 1.0, False),
    ("i_k", IDX_DIM, F32, F32, False, 1.0, False),
    ("a_qT", A_WIDTH, BF16, BF16, True, A_HEAD_DIM ** -0.5 * LOG2E, False),
    ("a_vT", A_WIDTH, BF16, BF16, True, 1.0, True),
    ("b_qT", B_WIDTH, BF16, BF16, True, B_QK_DIM ** -0.5 * LOG2E, False),
    ("b_vT", B_WIDTH, BF16, BF16, True, 1.0, True),
    ("i_qT", IDX_HEADS * IDX_DIM, F32, F32, True, 1.0, False),
    ("i_wT", IDX_HEADS, F32, F32, True, 1.0, False),
)
_W_GROUPS = ((BF16, False), (BF16, True), (F32, False), (F32, True))


def _proj_kernel(x_ref, *refs):
    w_refs = dict(zip(_W_GROUPS, refs[:len(_W_GROUPS)]))
    out_refs = refs[len(_W_GROUPS):]
    x = {F32: x_ref[...]}
    x[BF16] = x[F32].astype(BF16)
    off = {g: 0 for g in _W_GROUPS}
    for o_ref, (_, width, dtype, op_dtype, feature_major, scale, chunked) in zip(out_refs, _OUTS):
        g = (op_dtype, feature_major)
        lo = off[g]
        off[g] = lo + width
        if feature_major:
            r = _nt_dot(w_refs[g][lo:lo + width, :], x[op_dtype])
        else:
            r = _dot(x[op_dtype], w_refs[g][:, lo:lo + width])
        if scale != 1.0:
            r = r * scale
        if chunked:
            o_ref[0] = r.astype(dtype)
        else:
            o_ref[...] = r.astype(dtype)


def _project(x2, w):
    L = x2.shape[0]
    pts = [0]
    for s in IN_SIZES:
        pts.append(pts[-1] + s)
    names = ("a_q", "a_k", "a_v", "a_z", "i_q", "i_k", "i_w", "b_q", "b_k", "b_v", "b_z", "g_a", "g_b")
    col = {n: w[:, pts[k]:pts[k + 1]] for k, n in enumerate(names)}
    weights = []
    for op_dtype, feature_major in _W_GROUPS:
        cols = [col[o[0][:-1] if feature_major else o[0]] for o in _OUTS if (o[3], o[4]) == (op_dtype, feature_major)]
        wg = jnp.concatenate(cols, axis=1).astype(op_dtype)
        weights.append(wg.T if feature_major else wg)
    rows = PROJ_ROWS
    out_shape, out_specs = [], []
    for _, width, dtype, _, feature_major, _, chunked in _OUTS:
        if chunked:
            out_shape.append(jax.ShapeDtypeStruct((L // rows, width, rows), dtype))
            out_specs.append(pl.BlockSpec((1, width, rows), lambda i: (i, 0, 0)))
        elif feature_major:
            out_shape.append(jax.ShapeDtypeStruct((width, L), dtype))
            out_specs.append(pl.BlockSpec((width, rows), lambda i: (0, i)))
        else:
            out_shape.append(jax.ShapeDtypeStruct((L, width), dtype))
            out_specs.append(pl.BlockSpec((rows, width), lambda i: (i, 0)))
    outs = pl.pallas_call(
        _proj_kernel,
        grid=(L // rows,),
        in_specs=[pl.BlockSpec((rows, D_MODEL), lambda i: (i, 0))] + [_resident(wg.shape) for wg in weights],
        out_specs=out_specs,
        out_shape=out_shape,
        compiler_params=pltpu.CompilerParams(dimension_semantics=("arbitrary",),
                                             vmem_limit_bytes=VMEM_LIMIT),
        name="proj",
    )(x2, *weights)
    return {o[0]: r for o, r in zip(_OUTS, outs)}


def _rel_bucket(dist):
    n = jnp.maximum(dist, 0)
    max_exact = N_BUCKETS // 2
    nf = jnp.maximum(n, 1).astype(F32)
    large = max_exact + (jnp.log(nf / max_exact) / math.log(MAX_DISTANCE / max_exact)
                         * (N_BUCKETS - max_exact)).astype(I32)
    large = jnp.minimum(large, N_BUCKETS - 1)
    return jnp.where(n < max_exact, n, large)


def _bias_tiles(rel_bias):
    shifted = rel_bias - rel_bias[N_BUCKETS - 1]
    by_dist = shifted[_rel_bucket(jnp.arange(2 * KC + TQ - 1) - (KC - 1))]
    tiles = []
    for d in range(2):
        c = d * KC + KC - 1
        period = jnp.concatenate([by_dist[c:c + TQ], jnp.zeros_like(by_dist[:1]),
                                  by_dist[c - (KC - 1):c]], axis=0)
        flat = jnp.tile(period, (KC, 1))[:KC * (KC + TQ - 1)]
        tile = flat.reshape(KC, KC + TQ - 1, -1)[:, :TQ, :]
        tiles.append(jnp.moveaxis(tile, -1, 0))
    tiles.append(jnp.zeros_like(tiles[0]))
    return (jnp.stack(tiles) * LOG2E).astype(F32)


def _stage_logits(k_ref, qT_ref, s_scr, smax_scr, buf, c, n_streams, dim, extra=None):
    ks = c * KC if isinstance(c, int) else pl.multiple_of(c * KC, KC)
    for h in range(n_streams):
        lo = h * dim
        s = _dot(k_ref[pl.ds(ks, KC), lo:lo + dim], qT_ref[lo:lo + dim, :])
        if extra is not None:
            s = s + extra(h)
        s_scr[buf, h] = s
        smax_scr[buf, h] = jnp.max(s, axis=0, keepdims=True)


def _softmax_step(s_scr, smax_scr, buf, h, m_ref, acc_ref, vt):
    rows = acc_ref.shape[1]
    m_old = m_ref[h]
    m_new = jnp.maximum(m_old, smax_scr[buf, h])
    alpha = jnp.exp2(m_old - m_new)
    p = jnp.exp2(s_scr[buf, h] - m_new).astype(BF16)
    vt_ext = jnp.concatenate([vt, jnp.ones((BF16_ROWS, KC), BF16)], axis=0)
    acc_ref[h] = alpha * acc_ref[h] + _dot(vt_ext, p)[:rows]
    m_ref[h] = m_new


def _attend_step(d, stage, consume):
    @pl.when(d > SUB)
    def _far_step():
        stage(0, 0, False)
        for c in range(SUB):
            if c + 1 < SUB:
                stage((c + 1) % 2, c + 1, False)
            consume(c % 2, c, False)

    @pl.when((d >= 0) & (d <= SUB))
    def _edge_step():
        def body(c, carry):
            stage(0, c, True)
            consume(0, c, True)
            return carry

        lax.fori_loop(0, jnp.minimum(d + 1, SUB), body, 0)


def _silu(z):
    return z * jax.nn.sigmoid(z)


def _causal_steps(n_q_blocks):
    qi = [i for i in range(n_q_blocks) for _ in range(i // SUB + 1)]
    kj = [j for i in range(n_q_blocks) for j in range(i // SUB + 1)]
    return jnp.asarray(qi, I32), jnp.asarray(kj, I32)


def _q_blk(s, qi_ref, kj_ref):
    return qi_ref[s]


def _k_blk(s, qi_ref, kj_ref):
    return kj_ref[s]


def _score_chunk(ik_ref, iqT_ref, w, kc, causal_limit, plane_scr, eq_scr, sel_scr):
    kt = ik_ref[pl.ds(pl.multiple_of(kc * KC, KC), KC), :]
    sc = None
    for h in range(IDX_HEADS):
        dots = _dot(kt, iqT_ref[h * IDX_DIM:(h + 1) * IDX_DIM, :])
        term = w[h:h + 1, :] * jnp.maximum(dots, 0.0)
        sc = term if sc is None else sc + term
    sc = jnp.where(sc == 0.0, 0.0, sc)
    key = _order_key(lax.bitcast_convert_type(sc, I32))
    if causal_limit is not None:
        krow = lax.broadcasted_iota(I32, (KC, TQ), 0)
        qcol = lax.broadcasted_iota(I32, (KC, TQ), 1)
        key = jnp.where(krow <= causal_limit + qcol, key, 0)
    key = key.reshape(WORD_BITS, F32_ROWS, TQ)
    planes = _bit_transpose32([key[r] for r in range(WORD_BITS)])
    for r in range(WORD_BITS):
        plane_scr[kc, r] = planes[r]
    eq_scr[kc] = jnp.full((F32_ROWS, TQ), -1, I32)
    sel_scr[kc] = jnp.zeros((F32_ROWS, TQ), I32)


def _radix_select(i, plane_scr, eq_scr, sel_scr, sel_ref, topk, seq_len):
    n_chunks = sel_ref.shape[1]
    n_bits = max(1, (seq_len - 1).bit_length())
    n_groups = (i + RADIX_GROUP) // RADIX_GROUP

    def clear_chunk(c, carry):
        plane_scr[c] = jnp.zeros((WORD_BITS, F32_ROWS, TQ), I32)
        eq_scr[c] = jnp.zeros((F32_ROWS, TQ), I32)
        sel_scr[c] = jnp.zeros((F32_ROWS, TQ), I32)
        return carry

    lax.fori_loop(i + 1, n_groups * RADIX_GROUP, clear_chunk, 0)

    def count_bits(word_fn):
        def body(g, acc):
            for u in range(RADIX_GROUP):
                acc = acc + lax.population_count(word_fn(g * RADIX_GROUP + u))
            return acc
        acc = lax.fori_loop(0, n_groups, body, jnp.zeros((F32_ROWS, TQ), I32))
        return jnp.sum(acc, axis=0, keepdims=True)

    def apply_decision(c, b_prev, drop):
        e = eq_scr[c]
        p = plane_scr[c, b_prev]
        sel_scr[c] = sel_scr[c] | (e & p & drop)
        e = e & (p ^ drop)
        eq_scr[c] = e
        return e

    def decide(cnt_gt, ones):
        take = cnt_gt + ones >= topk
        return jnp.where(take, cnt_gt, cnt_gt + ones), jnp.where(take, 0, -1).astype(I32)

    def radix_step(b, carry):
        cnt_gt, drop = carry
        ones = count_bits(lambda c: apply_decision(c, b - 1, drop) & plane_scr[c, b])
        return decide(cnt_gt, ones)

    first = decide(jnp.zeros((1, TQ), I32), count_bits(lambda c: plane_scr[c, 0]))
    cnt_gt, drop = lax.fori_loop(1, WORD_BITS, radix_step, first)
    cnt_eq = count_bits(lambda c: apply_decision(c, WORD_BITS - 1, drop))

    n_causal = i * TQ + 1 + lax.broadcasted_iota(I32, (1, TQ), 1)
    full = n_causal >= topk
    need = topk - cnt_gt
    trim = full & (cnt_eq > need)
    jx0 = jnp.where(full, seq_len, 0).astype(I32)

    def tie_search(_):
        def idx_step(b, jx):
            cand = jx | (jnp.int32(1) << (n_bits - 1 - b))
            cnt = count_bits(lambda c: eq_scr[c] & _rows_below(cand - c * KC))
            return jnp.where(cnt < need, cand, jx)
        jx = lax.fori_loop(0, n_bits, idx_step, jnp.zeros((1, TQ), I32))
        return jnp.where(trim, jx + 1, jx0)

    jx = lax.cond(jnp.max(trim.astype(I32)) > 0, tie_search, lambda _: jx0, 0)

    def emit(c, carry):
        sel_ref[0, c] = sel_scr[c] | (eq_scr[c] & _rows_below(jx - c * KC))
        return carry

    lax.fori_loop(0, i + 1, emit, 0)

    def emit_empty(c, carry):
        sel_ref[0, c] = jnp.zeros((F32_ROWS, TQ), I32)
        return carry

    lax.fori_loop(i + 1, n_chunks, emit_empty, 0)


def _diff_kernel(qi_ref, kj_ref, qT_ref, k_ref, vT_ref, z_ref, bias_ref, lam_ref, subw_ref,
                 iqT_ref, iwT_ref, ik_ref, y_ref, sel_ref,
                 m_scr, acc_scr, s_scr, smax_scr, plane_scr, eq_scr, sel_scr, *, lam_init, topk, seq_len):
    i = qi_ref[pl.program_id(0)]
    j = kj_ref[pl.program_id(0)]
    d = i - j * SUB

    @pl.when(j == 0)
    def _init():
        m_scr[...] = jnp.full(m_scr.shape, NEG, F32)
        acc_scr[...] = jnp.zeros(acc_scr.shape, F32)

    w = iwT_ref[...] * (IDX_HEADS ** -0.5 * IDX_DIM ** -0.5)

    def stage(buf, c, edge):
        extra = None
        if edge:
            krow = lax.broadcasted_iota(I32, (KC, TQ), 0)
            qcol = lax.broadcasted_iota(I32, (KC, TQ), 1)
            addmask = jnp.where(krow <= (d - c) * KC + qcol, 0.0, NEG)
            bidx = jnp.minimum(d - c, 2)
            extra = lambda mi: bias_ref[bidx, mi // 2] + addmask
        _stage_logits(k_ref, qT_ref, s_scr, smax_scr, buf, c, B_MAPS, B_QK_DIM, extra)

    def consume(buf, c, edge):
        for mi in range(B_MAPS):
            hb = mi // 2
            _softmax_step(s_scr, smax_scr, buf, mi, m_scr, acc_scr,
                          vT_ref[c, hb * B_V_DIM:(hb + 1) * B_V_DIM, :])
        _score_chunk(ik_ref, iqT_ref, w, j * SUB + c, (d - c) * KC if edge else None,
                     plane_scr, eq_scr, sel_scr)

    _attend_step(d, stage, consume)

    @pl.when(j == i // SUB)
    def _finish():
        lq = lam_ref[...]
        lam = (jnp.exp(jnp.sum(lq[0:1, :] * lq[1:2, :], axis=1, keepdims=True))
               - jnp.exp(jnp.sum(lq[2:3, :] * lq[3:4, :], axis=1, keepdims=True)) + lam_init)
        outs = []
        for hb in range(B_HEADS):
            a1 = acc_scr[2 * hb]
            a2 = acc_scr[2 * hb + 1]
            o = (a1[:B_V_DIM] * (1.0 / a1[B_V_DIM:B_V_DIM + 1])
                 - lam * (a2[:B_V_DIM] * (1.0 / a2[B_V_DIM:B_V_DIM + 1])))
            ms = jnp.mean(o * o, axis=0, keepdims=True)
            outs.append(o * lax.rsqrt(ms + RMS_EPS) * subw_ref[...] * (1.0 - lam_init))
        o = jnp.concatenate(outs, axis=0).T
        y_ref[...] = (o * _silu(z_ref[...])).astype(y_ref.dtype)
        _radix_select(i, plane_scr, eq_scr, sel_scr, sel_ref, topk, seq_len)


def _diff_mixer(pr, bias_tiles, lambda_qk, subln_w, lam_init, L):
    topk = min(TOPK_MAX, L // 4)
    n_chunks = L // KC
    qi, kj = _causal_steps(L // TQ)
    grid_spec = pltpu.PrefetchScalarGridSpec(
        num_scalar_prefetch=2,
        grid=(qi.shape[0],),
        in_specs=[
            pl.BlockSpec((B_WIDTH, TQ), lambda *s: (0, _q_blk(*s))),
            pl.BlockSpec((TK, B_WIDTH), lambda *s: (_k_blk(*s), 0)),
            pl.BlockSpec((SUB, B_WIDTH, KC), lambda *s: (_k_blk(*s), 0, 0)),
            pl.BlockSpec((TQ, B_WIDTH), lambda *s: (_q_blk(*s), 0)),
            _resident(bias_tiles.shape),
            _resident(lambda_qk.shape),
            _resident(subln_w.shape),
            pl.BlockSpec((IDX_HEADS * IDX_DIM, TQ), lambda *s: (0, _q_blk(*s))),
            pl.BlockSpec((IDX_HEADS, TQ), lambda *s: (0, _q_blk(*s))),
            _resident((L, IDX_DIM)),
        ],
        out_specs=[
            pl.BlockSpec((TQ, B_WIDTH), lambda *s: (_q_blk(*s), 0)),
            pl.BlockSpec((1, n_chunks, F32_ROWS, TQ), lambda *s: (_q_blk(*s), 0, 0, 0)),
        ],
        scratch_shapes=[
            pltpu.VMEM((B_MAPS, 1, TQ), F32),
            pltpu.VMEM((B_MAPS, B_V_DIM + F32_ROWS, TQ), F32),
            pltpu.VMEM((2, B_MAPS, KC, TQ), F32),
            pltpu.VMEM((2, B_MAPS, 1, TQ), F32),
            pltpu.VMEM((n_chunks, WORD_BITS, F32_ROWS, TQ), I32),
            pltpu.VMEM((n_chunks, F32_ROWS, TQ), I32),
            pltpu.VMEM((n_chunks, F32_ROWS, TQ), I32),
        ],
    )
    return pl.pallas_call(
        functools.partial(_diff_kernel, lam_init=lam_init, topk=topk, seq_len=L),
        grid_spec=grid_spec,
        out_shape=[jax.ShapeDtypeStruct((L, B_WIDTH), BF16),
                   jax.ShapeDtypeStruct((L // TQ, n_chunks, F32_ROWS, TQ), I32)],
        compiler_params=pltpu.CompilerParams(dimension_semantics=("arbitrary",),
                                             vmem_limit_bytes=VMEM_LIMIT),
        name="diff",
    )(qi, kj, pr["b_qT"], pr["b_k"], pr["b_vT"], pr["b_z"], bias_tiles, lambda_qk, subln_w,
      pr["i_qT"], pr["i_wT"], pr["i_k"])


def _dsa_kernel(qi_ref, kj_ref, qT_ref, k_ref, vT_ref, z_ref, bias_ref, sel_ref, y_ref,
                m_scr, acc_scr, s_scr, smax_scr):
    i = qi_ref[pl.program_id(0)]
    j = kj_ref[pl.program_id(0)]
    d = i - j * SUB

    @pl.when(j == 0)
    def _init():
        m_scr[...] = jnp.full(m_scr.shape, NEG, F32)
        acc_scr[...] = jnp.zeros(acc_scr.shape, F32)

    def stage(buf, c, edge):
        sel = sel_ref[0, c]
        addmask = jnp.concatenate(
            [jnp.where((sel << r) < 0, 0.0, NEG) for r in range(WORD_BITS)], axis=0)
        if edge:
            bidx = jnp.minimum(d - c, 2)
            extra = lambda h: bias_ref[bidx, h] + addmask
        else:
            extra = lambda h: addmask
        _stage_logits(k_ref, qT_ref, s_scr, smax_scr, buf, c, A_HEADS, A_HEAD_DIM, extra)

    def consume(buf, c, edge):
        for h in range(A_HEADS):
            lo = h * A_HEAD_DIM
            _softmax_step(s_scr, smax_scr, buf, h, m_scr, acc_scr, vT_ref[c, lo:lo + A_HEAD_DIM, :])

    _attend_step(d, stage, consume)

    @pl.when(j == i // SUB)
    def _finish():
        outs = []
        for h in range(A_HEADS):
            acc = acc_scr[h]
            outs.append(acc[:A_HEAD_DIM] * (1.0 / acc[A_HEAD_DIM:A_HEAD_DIM + 1]))
        o = jnp.concatenate(outs, axis=0).T
        y_ref[...] = (o * _silu(z_ref[...])).astype(y_ref.dtype)


def _dsa_mixer(pr, bias_tiles, sel_bits, L):
    qi, kj = _causal_steps(L // TQ)
    grid_spec = pltpu.PrefetchScalarGridSpec(
        num_scalar_prefetch=2,
        grid=(qi.shape[0],),
        in_specs=[
            pl.BlockSpec((A_WIDTH, TQ), lambda *s: (0, _q_blk(*s))),
            pl.BlockSpec((TK, A_WIDTH), lambda *s: (_k_blk(*s), 0)),
            pl.BlockSpec((SUB, A_WIDTH, KC), lambda *s: (_k_blk(*s), 0, 0)),
            pl.BlockSpec((TQ, A_WIDTH), lambda *s: (_q_blk(*s), 0)),
            _resident(bias_tiles.shape),
            pl.BlockSpec((1, SUB, F32_ROWS, TQ), lambda *s: (_q_blk(*s), _k_blk(*s), 0, 0)),
        ],
        out_specs=pl.BlockSpec((TQ, A_WIDTH), lambda *s: (_q_blk(*s), 0)),
        scratch_shapes=[
            pltpu.VMEM((A_HEADS, 1, TQ), F32),
            pltpu.VMEM((A_HEADS, A_HEAD_DIM + F32_ROWS, TQ), F32),
            pltpu.VMEM((2, A_HEADS, KC, TQ), F32),
            pltpu.VMEM((2, A_HEADS, 1, TQ), F32),
        ],
    )
    return pl.pallas_call(
        _dsa_kernel,
        grid_spec=grid_spec,
        out_shape=jax.ShapeDtypeStruct((L, A_WIDTH), BF16),
        compiler_params=pltpu.CompilerParams(dimension_semantics=("arbitrary",),
                                             vmem_limit_bytes=VMEM_LIMIT),
        name="dsa",
    )(qi, kj, pr["a_qT"], pr["a_k"], pr["a_vT"], pr["a_z"], bias_tiles, sel_bits)


def _merge_kernel(x_ref, p_ref, ya_ref, yb_ref, ga_ref, gb_ref, wpa_ref, wpb_ref, wo_ref,
                  lng_ref, lnb_ref, wple_ref, wgate_ref, out_ref):
    merged = (jax.nn.sigmoid(ga_ref[...]) * _dot(ya_ref[...], wpa_ref[...])
              + jax.nn.sigmoid(gb_ref[...]) * _dot(yb_ref[...], wpb_ref[...]))
    mix = _dot(merged.astype(BF16), wo_ref[...])
    z = DEEPNORM_ALPHA * x_ref[...] + mix
    mu = jnp.mean(z, axis=-1, keepdims=True)
    zc = z - mu
    var = jnp.mean(zc * zc, axis=-1, keepdims=True)
    x1 = zc * lax.rsqrt(var + LN_EPS) * lng_ref[...] + lnb_ref[...]
    gate = jax.nn.sigmoid(_dot(x1.astype(BF16), wgate_ref[...]))
    out_ref[...] = x1 + gate * _dot(p_ref[...].astype(BF16), wple_ref[...])


def _merge(x2, p2, y_a, y_b, g_a, g_b, w_pa, w_pb, w_o, ln_g, ln_b, w_ple, w_gate):
    L = x2.shape[0]
    rows = MERGE_ROWS
    row_blk = lambda width: pl.BlockSpec((rows, width), lambda i: (i, 0))
    weights = (w_pa.astype(BF16), w_pb.astype(BF16), w_o.astype(BF16),
               ln_g.reshape(1, D_MODEL), ln_b.reshape(1, D_MODEL), w_ple.astype(BF16), w_gate.astype(BF16))
    return pl.pallas_call(
        _merge_kernel,
        grid=(L // rows,),
        in_specs=[row_blk(D_MODEL), row_blk(PLE_DIM), row_blk(A_WIDTH), row_blk(B_WIDTH),
                  row_blk(D_MODEL), row_blk(D_MODEL)] + [_resident(a.shape) for a in weights],
        out_specs=row_blk(D_MODEL),
        out_shape=jax.ShapeDtypeStruct((L, D_MODEL), F32),
        compiler_params=pltpu.CompilerParams(dimension_semantics=("arbitrary",),
                                             vmem_limit_bytes=VMEM_LIMIT),
        name="merge",
    )(x2, p2, y_a, y_b, g_a, g_b, *weights)


def kernel(x, p, w_in, w_pa, w_pb, w_o, lambda_qk, subln_w, ln_g, ln_b, w_ple, w_ple_gate, rel_bias):
    b, L, _ = x.shape
    assert b == 1 and w_in.shape[0] == DEPTH == 1
    assert L % TK == 0 and TQ == KC and KC >= MAX_DISTANCE and KC == WORD_BITS * F32_ROWS
    assert (L // KC) % RADIX_GROUP == 0
    tiles = _bias_tiles(rel_bias)
    x2 = x[0]
    pr = _project(x2, w_in[0])
    lam_init = 0.8 - 0.6 * math.exp(-0.3 * 0)
    y_b, sel_bits = _diff_mixer(pr, tiles[:, A_HEADS:], lambda_qk[0], subln_w[0].reshape(B_V_DIM, 1),
                                lam_init, L)
    y_a = _dsa_mixer(pr, tiles[:, :A_HEADS], sel_bits, L)
    out = _merge(x2, p[0, 0], y_a, y_b, pr["g_a"], pr["g_b"], w_pa[0], w_pb[0], w_o[0],
                 ln_g[0], ln_b[0], w_ple[0], w_ple_gate[0])
    return out[None]
```

```python
import functools
import math

import jax
import jax.numpy as jnp
from jax import lax
from jax.experimental import pallas as pl
from jax.experimental.pallas import tpu as pltpu

D_MODEL = 1024
A_HEADS = 8
A_HEAD_DIM = 64
A_WIDTH = A_HEADS * A_HEAD_DIM
IDX_HEADS = 8
IDX_DIM = 64
TOPK_MAX = 256
B_HEADS = 4
B_QK_DIM = 64
B_MAPS = 2 * B_HEADS
B_V_DIM = 2 * B_QK_DIM
B_WIDTH = B_HEADS * B_V_DIM
N_BUCKETS = 32
MAX_DISTANCE = 128
PLE_DIM = 256
LN_EPS = 1e-5
RMS_EPS = 1e-5
DEPTH = 1
DEEPNORM_ALPHA = (2 * DEPTH) ** 0.25
LOG2E = math.log2(math.e)

IN_SIZES = (
    A_WIDTH, A_WIDTH, A_WIDTH, A_WIDTH,
    IDX_HEADS * IDX_DIM, IDX_DIM, IDX_HEADS,
    B_HEADS * 2 * B_QK_DIM, B_HEADS * 2 * B_QK_DIM,
    B_WIDTH, B_WIDTH,
    D_MODEL, D_MODEL,
)

BF16 = jnp.bfloat16
F32 = jnp.float32
I32 = jnp.int32

TQ = 256
KC = 256
SUB = 8
TK = SUB * KC
PROJ_ROWS = KC
MERGE_ROWS = 512
VMEM_LIMIT = 56 * 1024 * 1024
BF16_ROWS = 16
F32_ROWS = 8
WORD_BITS = 32
RADIX_GROUP = 8

NEG = -1e30
INT_MIN = -(2 ** 31)


def _nt_dot(a, b):
    return lax.dot_general(a, b, (((1,), (1,)), ((), ())), preferred_element_type=F32)


def _dot(a, b):
    return jnp.dot(a, b, preferred_element_type=F32)


def _order_key(bits):
    return bits ^ ((bits >> 31) | INT_MIN)


def _bit_transpose32(words):
    w = list(words)
    shift, mask = 16, 0x0000FFFF
    while shift:
        for k in range(WORD_BITS):
            if not k & shift:
                t = (w[k] ^ (w[k + shift] >> shift)) & mask
                w[k] = w[k] ^ t
                w[k + shift] = w[k + shift] ^ (t << shift)
        shift >>= 1
        mask = (mask ^ (mask << shift)) & 0xFFFFFFFF
    return w


def _rows_below(limit):
    sub = lax.broadcasted_iota(I32, (F32_ROWS, TQ), 0)
    n = jnp.clip((limit - sub + (F32_ROWS - 1)) >> 3, 0, WORD_BITS)
    return jnp.where(n > 0, jnp.int32(INT_MIN) >> (jnp.maximum(n, 1) - 1), 0)


def _resident(shape):
    return pl.BlockSpec(shape, lambda *_: (0,) * len(shape), pipeline_mode=pl.Buffered(1))


_OUTS = (
    ("a_k", A_WIDTH, BF16, BF16, False, 1.0, False),
    ("b_k", B_WIDTH, BF16, BF16, False, 1.0, False),
    ("a_z", A_WIDTH, F32, BF16, False, 1.0, False),
    ("b_z", B_WIDTH, F32, BF16, False, 1.0, False),
    ("g_a", D_MODEL, F32, BF16, False, 1.0, False),
    ("g_b", D_MODEL, F32, BF16, False, 1.0, False),
    ("i_k", IDX_DIM, F32, F32, False, 1.0, False),
    ("a_qT", A_WIDTH, BF16, BF16, True, A_HEAD_DIM ** -0.5 * LOG2E, False),
    ("a_vT", A_WIDTH, BF16, BF16, True, 1.0, True),
    ("b_qT", B_WIDTH, BF16, BF16, True, B_QK_DIM ** -0.5 * LOG2E, False),
    ("b_vT", B_WIDTH, BF16, BF16, True, 1.0, True),
    ("i_qT", IDX_HEADS * IDX_DIM, F32, F32, True, 1.0, False),
    ("i_wT", IDX_HEADS, F32, F32, True, 1.0, False),
)
_W_GROUPS = ((BF16, False), (BF16, True), (F32, False), (F32, True))


def _proj_kernel(x_ref, *refs):
    w_refs = dict(zip(_W_GROUPS, refs[:len(_W_GROUPS)]))
    out_refs = refs[len(_W_GROUPS):]
    x = {F32: x_ref[...]}
    x[BF16] = x[F32].astype(BF16)
    off = {g: 0 for g in _W_GROUPS}
    for o_ref, (_, width, dtype, op_dtype, feature_major, scale, chunked) in zip(out_refs, _OUTS):
        g = (op_dtype, feature_major)
        lo = off[g]
        off[g] = lo + width
        if feature_major:
            r = _nt_dot(w_refs[g][lo:lo + width, :], x[op_dtype])
        else:
            r = _dot(x[op_dtype], w_refs[g][:, lo:lo + width])
        if scale != 1.0:
            r = r * scale
        if chunked:
            o_ref[0] = r.astype(dtype)
        else:
            o_ref[...] = r.astype(dtype)


def _project(x2, w):
    L = x2.shape[0]
    pts = [0]
    for s in IN_SIZES:
        pts.append(pts[-1] + s)
    names = ("a_q", "a_k", "a_v", "a_z", "i_q", "i_k", "i_w", "b_q", "b_k", "b_v", "b_z", "g_a", "g_b")
    col = {n: w[:, pts[k]:pts[k + 1]] for k, n in enumerate(names)}
    weights = []
    for op_dtype, feature_major in _W_GROUPS:
        cols = [col[o[0][:-1] if feature_major else o[0]] for o in _OUTS if (o[3], o[4]) == (op_dtype, feature_major)]
        wg = jnp.concatenate(cols, axis=1).astype(op_dtype)
        weights.append(wg.T if feature_major else wg)
    rows = PROJ_ROWS
    out_shape, out_specs = [], []
    for _, width, dtype, _, feature_major, _, chunked in _OUTS:
        if chunked:
            out_shape.append(jax.ShapeDtypeStruct((L // rows, width, rows), dtype))
            out_specs.append(pl.BlockSpec((1, width, rows), lambda i: (i, 0, 0)))
        elif feature_major:
            out_shape.append(jax.ShapeDtypeStruct((width, L), dtype))
            out_specs.append(pl.BlockSpec((width, rows), lambda i: (0, i)))
        else:
            out_shape.append(jax.ShapeDtypeStruct((L, width), dtype))
            out_specs.append(pl.BlockSpec((rows, width), lambda i: (i, 0)))
    outs = pl.pallas_call(
        _proj_kernel,
        grid=(L // rows,),
        in_specs=[pl.BlockSpec((rows, D_MODEL), lambda i: (i, 0))] + [_resident(wg.shape) for wg in weights],
        out_specs=out_specs,
        out_shape=out_shape,
        compiler_params=pltpu.CompilerParams(dimension_semantics=("arbitrary",),
                                             vmem_limit_bytes=VMEM_LIMIT),
        name="proj",
    )(x2, *weights)
    return {o[0]: r for o, r in zip(_OUTS, outs)}


def _rel_bucket(dist):
    n = jnp.maximum(dist, 0)
    max_exact = N_BUCKETS // 2
    nf = jnp.maximum(n, 1).astype(F32)
    large = max_exact + (jnp.log(nf / max_exact) / math.log(MAX_DISTANCE / max_exact)
                         * (N_BUCKETS - max_exact)).astype(I32)
    large = jnp.minimum(large, N_BUCKETS - 1)
    return jnp.where(n < max_exact, n, large)


def _bias_tiles(rel_bias):
    shifted = rel_bias - rel_bias[N_BUCKETS - 1]
    by_dist = shifted[_rel_bucket(jnp.arange(2 * KC + TQ - 1) - (KC - 1))].T
    n_heads = by_dist.shape[0]
    tiles = []
    for d in range(2):
        c = d * KC + KC - 1
        period = jnp.concatenate([by_dist[:, c:c + TQ], jnp.zeros((n_heads, 1), F32),
                                  by_dist[:, c - (KC - 1):c]], axis=1)
        flat = jnp.tile(period, (1, KC))[:, :KC * (KC + TQ - 1)]
        tiles.append(flat.reshape(n_heads, KC, KC + TQ - 1)[:, :, :TQ])
    tiles.append(jnp.zeros_like(tiles[0]))
    return (jnp.stack(tiles) * LOG2E).astype(F32)


def _stage_logits(k_ref, qT_ref, s_scr, smax_scr, buf, c, n_streams, dim, extra=None):
    ks = c * KC if isinstance(c, int) else pl.multiple_of(c * KC, KC)
    for h in range(n_streams):
        lo = h * dim
        s = _dot(k_ref[pl.ds(ks, KC), lo:lo + dim], qT_ref[lo:lo + dim, :])
        if extra is not None:
            s = s + extra(h)
        s_scr[buf, h] = s
        smax_scr[buf, h] = jnp.max(s, axis=0, keepdims=True)


def _softmax_step(s_scr, smax_scr, buf, h, m_ref, acc_ref, vt):
    rows = acc_ref.shape[1]
    m_old = m_ref[h]
    m_new = jnp.maximum(m_old, smax_scr[buf, h])
    alpha = jnp.exp2(m_old - m_new)
    p = jnp.exp2(s_scr[buf, h] - m_new).astype(BF16)
    vt_ext = jnp.concatenate([vt, jnp.ones((BF16_ROWS, KC), BF16)], axis=0)
    acc_ref[h] = alpha * acc_ref[h] + _dot(vt_ext, p)[:rows]
    m_ref[h] = m_new


def _attend_step(d, stage, consume):
    @pl.when(d > SUB)
    def _far_step():
        stage(0, 0, False)
        for c in range(SUB):
            if c + 1 < SUB:
                stage((c + 1) % 2, c + 1, False)
            consume(c % 2, c, False)

    @pl.when((d >= 0) & (d <= SUB))
    def _edge_step():
        def body(c, carry):
            stage(0, c, True)
            consume(0, c, True)
            return carry

        lax.fori_loop(0, jnp.minimum(d + 1, SUB), body, 0)


def _silu(z):
    return z * jax.nn.sigmoid(z)


def _causal_steps(n_q_blocks):
    qi = [i for i in range(n_q_blocks) for _ in range(i // SUB + 1)]
    kj = [j for i in range(n_q_blocks) for j in range(i // SUB + 1)]
    return jnp.asarray(qi, I32), jnp.asarray(kj, I32)


def _q_blk(s, qi_ref, kj_ref):
    return qi_ref[s]


def _k_blk(s, qi_ref, kj_ref):
    return kj_ref[s]


def _score_chunk(ik_ref, iqT_ref, w, kc, causal_limit, plane_scr, eq_scr, sel_scr):
    kt = ik_ref[pl.ds(pl.multiple_of(kc * KC, KC), KC), :]
    sc = None
    for h in range(IDX_HEADS):
        dots = _dot(kt, iqT_ref[h * IDX_DIM:(h + 1) * IDX_DIM, :])
        term = w[h:h + 1, :] * jnp.maximum(dots, 0.0)
        sc = term if sc is None else sc + term
    sc = jnp.where(sc == 0.0, 0.0, sc)
    key = _order_key(lax.bitcast_convert_type(sc, I32))
    if causal_limit is not None:
        krow = lax.broadcasted_iota(I32, (KC, TQ), 0)
        qcol = lax.broadcasted_iota(I32, (KC, TQ), 1)
        key = jnp.where(krow <= causal_limit + qcol, key, 0)
    key = key.reshape(WORD_BITS, F32_ROWS, TQ)
    planes = _bit_transpose32([key[r] for r in range(WORD_BITS)])
    for r in range(WORD_BITS):
        plane_scr[kc, r] = planes[r]
    eq_scr[kc] = jnp.full((F32_ROWS, TQ), -1, I32)
    sel_scr[kc] = jnp.zeros((F32_ROWS, TQ), I32)


def _radix_select(i, plane_scr, eq_scr, sel_scr, sel_ref, topk, seq_len):
    n_chunks = sel_ref.shape[1]
    n_groups = (i + RADIX_GROUP) // RADIX_GROUP

    def clear_chunk(c, carry):
        plane_scr[c] = jnp.zeros((WORD_BITS, F32_ROWS, TQ), I32)
        eq_scr[c] = jnp.zeros((F32_ROWS, TQ), I32)
        sel_scr[c] = jnp.zeros((F32_ROWS, TQ), I32)
        return carry

    lax.fori_loop(i + 1, n_groups * RADIX_GROUP, clear_chunk, 0)

    def count_bits(word_fn):
        def body(g, acc):
            for u in range(RADIX_GROUP):
                acc = acc + lax.population_count(word_fn(g * RADIX_GROUP + u))
            return acc
        acc = lax.fori_loop(0, n_groups, body, jnp.zeros((F32_ROWS, TQ), I32))
        return jnp.sum(acc, axis=0, keepdims=True)

    def apply_decision(c, b_prev, drop):
        e = eq_scr[c]
        p = plane_scr[c, b_prev]
        sel_scr[c] = sel_scr[c] | (e & p & drop)
        e = e & (p ^ drop)
        eq_scr[c] = e
        return e

    def decide(cnt_gt, ones):
        take = cnt_gt + ones >= topk
        return jnp.where(take, cnt_gt, cnt_gt + ones), jnp.where(take, 0, -1).astype(I32)

    def radix_step(b, carry):
        cnt_gt, drop = carry
        ones = count_bits(lambda c: apply_decision(c, b - 1, drop) & plane_scr[c, b])
        return decide(cnt_gt, ones)

    first = decide(jnp.zeros((1, TQ), I32), count_bits(lambda c: plane_scr[c, 0]))
    cnt_gt, drop = lax.fori_loop(1, WORD_BITS, radix_step, first)
    cnt_eq = count_bits(lambda c: apply_decision(c, WORD_BITS - 1, drop))

    n_causal = i * TQ + 1 + lax.broadcasted_iota(I32, (1, TQ), 1)
    full = n_causal >= topk
    need = topk - cnt_gt
    trim = full & (cnt_eq > need)
    jx0 = jnp.where(full, seq_len, 0).astype(I32)

    def tie_search(_):
        def locate(c, carry):
            seen, chunk, before = carry
            here = jnp.sum(lax.population_count(eq_scr[c]), axis=0, keepdims=True)
            hit = (seen < need) & (seen + here >= need)
            return seen + here, jnp.where(hit, c, chunk), jnp.where(hit, seen, before)

        zero = jnp.zeros((1, TQ), I32)
        _, chunk, before = lax.fori_loop(0, i + 1, locate, (zero, zero, zero))

        def pick(c, words):
            return words | jnp.where(chunk == c, eq_scr[c], 0)

        words = lax.fori_loop(0, i + 1, pick, jnp.zeros((F32_ROWS, TQ), I32))

        row = zero
        for b in reversed(range(max(1, (KC - 1).bit_length()))):
            cand = row | (1 << b)
            cnt = jnp.sum(lax.population_count(words & _rows_below(cand)), axis=0, keepdims=True)
            row = jnp.where(before + cnt < need, cand, row)
        return jnp.where(trim, chunk * KC + row + 1, jx0)

    jx = lax.cond(jnp.max(trim.astype(I32)) > 0, tie_search, lambda _: jx0, 0)

    def emit(c, carry):
        sel_ref[0, c] = sel_scr[c] | (eq_scr[c] & _rows_below(jx - c * KC))
        return carry

    lax.fori_loop(0, i + 1, emit, 0)

    def emit_empty(c, carry):
        sel_ref[0, c] = jnp.zeros((F32_ROWS, TQ), I32)
        return carry

    lax.fori_loop(i + 1, n_chunks, emit_empty, 0)


def _diff_kernel(qi_ref, kj_ref, qT_ref, k_ref, vT_ref, z_ref, bias_ref, lam_ref, subw_ref,
                 iqT_ref, iwT_ref, ik_ref, y_ref, sel_ref,
                 m_scr, acc_scr, s_scr, smax_scr, plane_scr, eq_scr, sel_scr, *, lam_init, topk, seq_len):
    i = qi_ref[pl.program_id(0)]
    j = kj_ref[pl.program_id(0)]
    d = i - j * SUB

    @pl.when(j == 0)
    def _init():
        m_scr[...] = jnp.full(m_scr.shape, NEG, F32)
        acc_scr[...] = jnp.zeros(acc_scr.shape, F32)

    w = iwT_ref[...] * (IDX_HEADS ** -0.5 * IDX_DIM ** -0.5)

    def stage(buf, c, edge):
        extra = None
        if edge:
            krow = lax.broadcasted_iota(I32, (KC, TQ), 0)
            qcol = lax.broadcasted_iota(I32, (KC, TQ), 1)
            addmask = jnp.where(krow <= (d - c) * KC + qcol, 0.0, NEG)
            bidx = jnp.minimum(d - c, 2)
            extra = lambda mi: bias_ref[bidx, mi // 2] + addmask
        _stage_logits(k_ref, qT_ref, s_scr, smax_scr, buf, c, B_MAPS, B_QK_DIM, extra)

    def consume(buf, c, edge):
        for mi in range(B_MAPS):
            hb = mi // 2
            _softmax_step(s_scr, smax_scr, buf, mi, m_scr, acc_scr,
                          vT_ref[c, hb * B_V_DIM:(hb + 1) * B_V_DIM, :])
        _score_chunk(ik_ref, iqT_ref, w, j * SUB + c, (d - c) * KC if edge else None,
                     plane_scr, eq_scr, sel_scr)

    _attend_step(d, stage, consume)

    @pl.when(j == i // SUB)
    def _finish():
        lq = lam_ref[...]
        lam = (jnp.exp(jnp.sum(lq[0:1, :] * lq[1:2, :], axis=1, keepdims=True))
               - jnp.exp(jnp.sum(lq[2:3, :] * lq[3:4, :], axis=1, keepdims=True)) + lam_init)
        outs = []
        for hb in range(B_HEADS):
            a1 = acc_scr[2 * hb]
            a2 = acc_scr[2 * hb + 1]
            o = (a1[:B_V_DIM] * (1.0 / a1[B_V_DIM:B_V_DIM + 1])
                 - lam * (a2[:B_V_DIM] * (1.0 / a2[B_V_DIM:B_V_DIM + 1])))
            ms = jnp.mean(o * o, axis=0, keepdims=True)
            outs.append(o * lax.rsqrt(ms + RMS_EPS) * subw_ref[...] * (1.0 - lam_init))
        o = jnp.concatenate(outs, axis=0).T
        y_ref[...] = (o * _silu(z_ref[...])).astype(y_ref.dtype)
        _radix_select(i, plane_scr, eq_scr, sel_scr, sel_ref, topk, seq_len)


def _diff_mixer(pr, bias_tiles, lambda_qk, subln_w, lam_init, L):
    topk = min(TOPK_MAX, L // 4)
    n_chunks = L // KC
    qi, kj = _causal_steps(L // TQ)
    grid_spec = pltpu.PrefetchScalarGridSpec(
        num_scalar_prefetch=2,
        grid=(qi.shape[0],),
        in_specs=[
            pl.BlockSpec((B_WIDTH, TQ), lambda *s: (0, _q_blk(*s))),
            pl.BlockSpec((TK, B_WIDTH), lambda *s: (_k_blk(*s), 0)),
            pl.BlockSpec((SUB, B_WIDTH, KC), lambda *s: (_k_blk(*s), 0, 0)),
            pl.BlockSpec((TQ, B_WIDTH), lambda *s: (_q_blk(*s), 0)),
            _resident(bias_tiles.shape),
            _resident(lambda_qk.shape),
            _resident(subln_w.shape),
            pl.BlockSpec((IDX_HEADS * IDX_DIM, TQ), lambda *s: (0, _q_blk(*s))),
            pl.BlockSpec((IDX_HEADS, TQ), lambda *s: (0, _q_blk(*s))),
            _resident((L, IDX_DIM)),
        ],
        out_specs=[
            pl.BlockSpec((TQ, B_WIDTH), lambda *s: (_q_blk(*s), 0)),
            pl.BlockSpec((1, n_chunks, F32_ROWS, TQ), lambda *s: (_q_blk(*s), 0, 0, 0)),
        ],
        scratch_shapes=[
            pltpu.VMEM((B_MAPS, 1, TQ), F32),
            pltpu.VMEM((B_MAPS, B_V_DIM + F32_ROWS, TQ), F32),
            pltpu.VMEM((2, B_MAPS, KC, TQ), F32),
            pltpu.VMEM((2, B_MAPS, 1, TQ), F32),
            pltpu.VMEM((n_chunks, WORD_BITS, F32_ROWS, TQ), I32),
            pltpu.VMEM((n_chunks, F32_ROWS, TQ), I32),
            pltpu.VMEM((n_chunks, F32_ROWS, TQ), I32),
        ],
    )
    return pl.pallas_call(
        functools.partial(_diff_kernel, lam_init=lam_init, topk=topk, seq_len=L),
        grid_spec=grid_spec,
        out_shape=[jax.ShapeDtypeStruct((L, B_WIDTH), BF16),
                   jax.ShapeDtypeStruct((L // TQ, n_chunks, F32_ROWS, TQ), I32)],
        compiler_params=pltpu.CompilerParams(dimension_semantics=("arbitrary",),
                                             vmem_limit_bytes=VMEM_LIMIT),
        name="diff",
    )(qi, kj, pr["b_qT"], pr["b_k"], pr["b_vT"], pr["b_z"], bias_tiles, lambda_qk, subln_w,
      pr["i_qT"], pr["i_wT"], pr["i_k"])


def _dsa_kernel(qi_ref, kj_ref, qT_ref, k_ref, vT_ref, z_ref, bias_ref, sel_ref, y_ref,
                m_scr, acc_scr, s_scr, smax_scr):
    i = qi_ref[pl.program_id(0)]
    j = kj_ref[pl.program_id(0)]
    d = i - j * SUB

    @pl.when(j == 0)
    def _init():
        m_scr[...] = jnp.full(m_scr.shape, NEG, F32)
        acc_scr[...] = jnp.zeros(acc_scr.shape, F32)

    def stage(buf, c, edge):
        sel = sel_ref[0, c]
        addmask = jnp.concatenate(
            [jnp.where((sel << r) < 0, 0.0, NEG) for r in range(WORD_BITS)], axis=0)
        if edge:
            bidx = jnp.minimum(d - c, 2)
            extra = lambda h: bias_ref[bidx, h] + addmask
        else:
            extra = lambda h: addmask
        _stage_logits(k_ref, qT_ref, s_scr, smax_scr, buf, c, A_HEADS, A_HEAD_DIM, extra)

    def consume(buf, c, edge):
        for h in range(A_HEADS):
            lo = h * A_HEAD_DIM
            _softmax_step(s_scr, smax_scr, buf, h, m_scr, acc_scr, vT_ref[c, lo:lo + A_HEAD_DIM, :])

    _attend_step(d, stage, consume)

    @pl.when(j == i // SUB)
    def _finish():
        outs = []
        for h in range(A_HEADS):
            acc = acc_scr[h]
            outs.append(acc[:A_HEAD_DIM] * (1.0 / acc[A_HEAD_DIM:A_HEAD_DIM + 1]))
        o = jnp.concatenate(outs, axis=0).T
        y_ref[...] = (o * _silu(z_ref[...])).astype(y_ref.dtype)


def _dsa_mixer(pr, bias_tiles, sel_bits, L):
    qi, kj = _causal_steps(L // TQ)
    grid_spec = pltpu.PrefetchScalarGridSpec(
        num_scalar_prefetch=2,
        grid=(qi.shape[0],),
        in_specs=[
            pl.BlockSpec((A_WIDTH, TQ), lambda *s: (0, _q_blk(*s))),
            pl.BlockSpec((TK, A_WIDTH), lambda *s: (_k_blk(*s), 0)),
            pl.BlockSpec((SUB, A_WIDTH, KC), lambda *s: (_k_blk(*s), 0, 0)),
            pl.BlockSpec((TQ, A_WIDTH), lambda *s: (_q_blk(*s), 0)),
            _resident(bias_tiles.shape),
            pl.BlockSpec((1, SUB, F32_ROWS, TQ), lambda *s: (_q_blk(*s), _k_blk(*s), 0, 0)),
        ],
        out_specs=pl.BlockSpec((TQ, A_WIDTH), lambda *s: (_q_blk(*s), 0)),
        scratch_shapes=[
            pltpu.VMEM((A_HEADS, 1, TQ), F32),
            pltpu.VMEM((A_HEADS, A_HEAD_DIM + F32_ROWS, TQ), F32),
            pltpu.VMEM((2, A_HEADS, KC, TQ), F32),
            pltpu.VMEM((2, A_HEADS, 1, TQ), F32),
        ],
    )
    return pl.pallas_call(
        _dsa_kernel,
        grid_spec=grid_spec,
        out_shape=jax.ShapeDtypeStruct((L, A_WIDTH), BF16),
        compiler_params=pltpu.CompilerParams(dimension_semantics=("arbitrary",),
                                             vmem_limit_bytes=VMEM_LIMIT),
        name="dsa",
    )(qi, kj, pr["a_qT"], pr["a_k"], pr["a_vT"], pr["a_z"], bias_tiles, sel_bits)


def _merge_kernel(x_ref, p_ref, ya_ref, yb_ref, ga_ref, gb_ref, wpa_ref, wpb_ref, wo_ref,
                  lng_ref, lnb_ref, wple_ref, wgate_ref, out_ref):
    merged = (jax.nn.sigmoid(ga_ref[...]) * _dot(ya_ref[...], wpa_ref[...])
              + jax.nn.sigmoid(gb_ref[...]) * _dot(yb_ref[...], wpb_ref[...]))
    mix = _dot(merged.astype(BF16), wo_ref[...])
    z = DEEPNORM_ALPHA * x_ref[...] + mix
    mu = jnp.mean(z, axis=-1, keepdims=True)
    zc = z - mu
    var = jnp.mean(zc * zc, axis=-1, keepdims=True)
    x1 = zc * lax.rsqrt(var + LN_EPS) * lng_ref[...] + lnb_ref[...]
    gate = jax.nn.sigmoid(_dot(x1.astype(BF16), wgate_ref[...]))
    out_ref[...] = x1 + gate * _dot(p_ref[...].astype(BF16), wple_ref[...])


def _merge(x2, p2, y_a, y_b, g_a, g_b, w_pa, w_pb, w_o, ln_g, ln_b, w_ple, w_gate):
    L = x2.shape[0]
    rows = MERGE_ROWS
    row_blk = lambda width: pl.BlockSpec((rows, width), lambda i: (i, 0))
    weights = (w_pa.astype(BF16), w_pb.astype(BF16), w_o.astype(BF16),
               ln_g.reshape(1, D_MODEL), ln_b.reshape(1, D_MODEL), w_ple.astype(BF16), w_gate.astype(BF16))
    return pl.pallas_call(
        _merge_kernel,
        grid=(L // rows,),
        in_specs=[row_blk(D_MODEL), row_blk(PLE_DIM), row_blk(A_WIDTH), row_blk(B_WIDTH),
                  row_blk(D_MODEL), row_blk(D_MODEL)] + [_resident(a.shape) for a in weights],
        out_specs=row_blk(D_MODEL),
        out_shape=jax.ShapeDtypeStruct((L, D_MODEL), F32),
        compiler_params=pltpu.CompilerParams(dimension_semantics=("arbitrary",),
                                             vmem_limit_bytes=VMEM_LIMIT),
        name="merge",
    )(x2, p2, y_a, y_b, g_a, g_b, *weights)


def kernel(x, p, w_in, w_pa, w_pb, w_o, lambda_qk, subln_w, ln_g, ln_b, w_ple, w_ple_gate, rel_bias):
    b, L, _ = x.shape
    assert b == 1 and w_in.shape[0] == DEPTH == 1
    assert L % TK == 0 and TQ == KC and KC >= MAX_DISTANCE and KC == WORD_BITS * F32_ROWS
    assert (L // KC) % RADIX_GROUP == 0
    tiles = _bias_tiles(rel_bias)
    x2 = x[0]
    pr = _project(x2, w_in[0])
    lam_init = 0.8 - 0.6 * math.exp(-0.3 * 0)
    y_b, sel_bits = _diff_mixer(pr, tiles[:, A_HEADS:], lambda_qk[0], subln_w[0].reshape(B_V_DIM, 1),
                                lam_init, L)
    y_a = _dsa_mixer(pr, tiles[:, :A_HEADS], sel_bits, L)
    out = _merge(x2, p[0, 0], y_a, y_b, pr["g_a"], pr["g_b"], w_pa[0], w_pb[0], w_o[0],
                 ln_g[0], ln_b[0], w_ple[0], w_ple_gate[0])
    return out[None]
```

```python
import functools
import math

import jax
import jax.numpy as jnp
from jax import lax
from jax.experimental import pallas as pl
from jax.experimental.pallas import tpu as pltpu

D_MODEL = 1024
A_HEADS = 8
A_HEAD_DIM = 64
A_WIDTH = A_HEADS * A_HEAD_DIM
IDX_HEADS = 8
IDX_DIM = 64
TOPK_MAX = 256
B_HEADS = 4
B_QK_DIM = 64
B_MAPS = 2 * B_HEADS
B_V_DIM = 2 * B_QK_DIM
B_WIDTH = B_HEADS * B_V_DIM
N_BUCKETS = 32
MAX_DISTANCE = 128
PLE_DIM = 256
LN_EPS = 1e-5
RMS_EPS = 1e-5
DEPTH = 1
DEEPNORM_ALPHA = (2 * DEPTH) ** 0.25
LOG2E = math.log2(math.e)

IN_SIZES = (
    A_WIDTH, A_WIDTH, A_WIDTH, A_WIDTH,
    IDX_HEADS * IDX_DIM, IDX_DIM, IDX_HEADS,
    B_HEADS * 2 * B_QK_DIM, B_HEADS * 2 * B_QK_DIM,
    B_WIDTH, B_WIDTH,
    D_MODEL, D_MODEL,
)

BF16 = jnp.bfloat16
F32 = jnp.float32
I32 = jnp.int32

TQ = 256
KC = 256
SUB = 8
TK = SUB * KC
PROJ_ROWS = KC
MERGE_ROWS = 512
VMEM_LIMIT = 56 * 1024 * 1024
BF16_ROWS = 16
F32_ROWS = 8
WORD_BITS = 32
RADIX_GROUP = 8

NEG = -1e30
INT_MIN = -(2 ** 31)


def _nt_dot(a, b):
    return lax.dot_general(a, b, (((1,), (1,)), ((), ())), preferred_element_type=F32)


def _dot(a, b):
    return jnp.dot(a, b, preferred_element_type=F32)


def _order_key(bits):
    return bits ^ ((bits >> 31) | INT_MIN)


def _bit_transpose32(words):
    w = list(words)
    shift, mask = 16, 0x0000FFFF
    while shift:
        for k in range(WORD_BITS):
            if not k & shift:
                t = (w[k] ^ (w[k + shift] >> shift)) & mask
                w[k] = w[k] ^ t
                w[k + shift] = w[k + shift] ^ (t << shift)
        shift >>= 1
        mask = (mask ^ (mask << shift)) & 0xFFFFFFFF
    return w


def _rows_below(limit):
    sub = lax.broadcasted_iota(I32, (F32_ROWS, TQ), 0)
    n = jnp.clip((limit - sub + (F32_ROWS - 1)) >> 3, 0, WORD_BITS)
    return jnp.where(n > 0, jnp.int32(INT_MIN) >> (jnp.maximum(n, 1) - 1), 0)


def _resident(shape):
    return pl.BlockSpec(shape, lambda *_: (0,) * len(shape), pipeline_mode=pl.Buffered(1))


_OUTS = (
    ("a_k", A_WIDTH, BF16, BF16, False, 1.0, False),
    ("b_k", B_WIDTH, BF16, BF16, False, 1.0, False),
    ("a_z", A_WIDTH, F32, BF16, False, 1.0, False),
    ("b_z", B_WIDTH, F32, BF16, False, 1.0, False),
    ("g_a", D_MODEL, F32, BF16, False, 1.0, False),
    ("g_b", D_MODEL, F32, BF16, False, 1.0, False),
    ("i_k", IDX_DIM, F32, F32, False, 1.0, False),
    ("a_qT", A_WIDTH, BF16, BF16, True, A_HEAD_DIM ** -0.5 * LOG2E, False),
    ("a_vT", A_WIDTH, BF16, BF16, True, 1.0, True),
    ("b_qT", B_WIDTH, BF16, BF16, True, B_QK_DIM ** -0.5 * LOG2E, False),
    ("b_vT", B_WIDTH, BF16, BF16, True, 1.0, True),
    ("i_qT", IDX_HEADS * IDX_DIM, F32, F32, True, 1.0, False),
    ("i_wT", IDX_HEADS, F32, F32, True, 1.0, False),
)
_W_GROUPS = ((BF16, False), (BF16, True), (F32, False), (F32, True))


def _proj_kernel(x_ref, *refs):
    w_refs = dict(zip(_W_GROUPS, refs[:len(_W_GROUPS)]))
    out_refs = refs[len(_W_GROUPS):]
    x = {F32: x_ref[...]}
    x[BF16] = x[F32].astype(BF16)
    off = {g: 0 for g in _W_GROUPS}
    for o_ref, (_, width, dtype, op_dtype, feature_major, scale, chunked) in zip(out_refs, _OUTS):
        g = (op_dtype, feature_major)
        lo = off[g]
        off[g] = lo + width
        if feature_major:
            r = _nt_dot(w_refs[g][lo:lo + width, :], x[op_dtype])
        else:
            r = _dot(x[op_dtype], w_refs[g][:, lo:lo + width])
        if scale != 1.0:
            r = r * scale
        if chunked:
            o_ref[0] = r.astype(dtype)
        else:
            o_ref[...] = r.astype(dtype)


def _project(x2, w):
    L = x2.shape[0]
    pts = [0]
    for s in IN_SIZES:
        pts.append(pts[-1] + s)
    names = ("a_q", "a_k", "a_v", "a_z", "i_q", "i_k", "i_w", "b_q", "b_k", "b_v", "b_z", "g_a", "g_b")
    col = {n: w[:, pts[k]:pts[k + 1]] for k, n in enumerate(names)}
    weights = []
    for op_dtype, feature_major in _W_GROUPS:
        cols = [col[o[0][:-1] if feature_major else o[0]] for o in _OUTS if (o[3], o[4]) == (op_dtype, feature_major)]
        wg = jnp.concatenate(cols, axis=1).astype(op_dtype)
        weights.append(wg.T if feature_major else wg)
    rows = PROJ_ROWS
    out_shape, out_specs = [], []
    for _, width, dtype, _, feature_major, _, chunked in _OUTS:
        if chunked:
            out_shape.append(jax.ShapeDtypeStruct((L // rows, width, rows), dtype))
            out_specs.append(pl.BlockSpec((1, width, rows), lambda i: (i, 0, 0)))
        elif feature_major:
            out_shape.append(jax.ShapeDtypeStruct((width, L), dtype))
            out_specs.append(pl.BlockSpec((width, rows), lambda i: (0, i)))
        else:
            out_shape.append(jax.ShapeDtypeStruct((L, width), dtype))
            out_specs.append(pl.BlockSpec((rows, width), lambda i: (i, 0)))
    outs = pl.pallas_call(
        _proj_kernel,
        grid=(L // rows,),
        in_specs=[pl.BlockSpec((rows, D_MODEL), lambda i: (i, 0))] + [_resident(wg.shape) for wg in weights],
        out_specs=out_specs,
        out_shape=out_shape,
        compiler_params=pltpu.CompilerParams(dimension_semantics=("arbitrary",),
                                             vmem_limit_bytes=VMEM_LIMIT),
        name="proj",
    )(x2, *weights)
    return {o[0]: r for o, r in zip(_OUTS, outs)}


def _rel_bucket(dist):
    n = jnp.maximum(dist, 0)
    max_exact = N_BUCKETS // 2
    nf = jnp.maximum(n, 1).astype(F32)
    large = max_exact + (jnp.log(nf / max_exact) / math.log(MAX_DISTANCE / max_exact)
                         * (N_BUCKETS - max_exact)).astype(I32)
    large = jnp.minimum(large, N_BUCKETS - 1)
    return jnp.where(n < max_exact, n, large)


def _bias_tiles(rel_bias):
    shifted = rel_bias - rel_bias[N_BUCKETS - 1]
    by_dist = shifted[_rel_bucket(jnp.arange(2 * KC + TQ - 1) - (KC - 1))].T
    tiles = []
    for d in range(2):
        c = d * KC + KC - 1
        rows = jax.vmap(lambda start: lax.dynamic_slice_in_dim(by_dist, start, TQ, axis=1))(
            c - jnp.arange(KC))
        tiles.append(jnp.transpose(rows, (1, 0, 2)))
    tiles.append(jnp.zeros_like(tiles[0]))
    return (jnp.stack(tiles) * LOG2E).astype(F32)


def _stage_logits(k_ref, qT_ref, s_scr, smax_scr, buf, c, n_streams, dim, extra=None):
    ks = c * KC if isinstance(c, int) else pl.multiple_of(c * KC, KC)
    for h in range(n_streams):
        lo = h * dim
        s = _dot(k_ref[pl.ds(ks, KC), lo:lo + dim], qT_ref[lo:lo + dim, :])
        if extra is not None:
            s = s + extra(h)
        s_scr[buf, h] = s
        smax_scr[buf, h] = jnp.max(s, axis=0, keepdims=True)


def _softmax_step(s_scr, smax_scr, buf, h, m_ref, acc_ref, vt):
    rows = acc_ref.shape[1]
    m_old = m_ref[h]
    m_new = jnp.maximum(m_old, smax_scr[buf, h])
    alpha = jnp.exp2(m_old - m_new)
    p = jnp.exp2(s_scr[buf, h] - m_new).astype(BF16)
    vt_ext = jnp.concatenate([vt, jnp.ones((BF16_ROWS, KC), BF16)], axis=0)
    acc_ref[h] = alpha * acc_ref[h] + _dot(vt_ext, p)[:rows]
    m_ref[h] = m_new


def _attend_step(d, stage, consume):
    @pl.when(d > SUB)
    def _far_step():
        stage(0, 0, False)
        for c in range(SUB):
            if c + 1 < SUB:
                stage((c + 1) % 2, c + 1, False)
            consume(c % 2, c, False)

    @pl.when((d >= 0) & (d <= SUB))
    def _edge_step():
        n_valid = jnp.minimum(d + 1, SUB)

        def pair(p, carry):
            stage(0, 2 * p, True)
            stage(1, 2 * p + 1, True)
            consume(0, 2 * p, True)
            consume(1, 2 * p + 1, True)
            return carry

        lax.fori_loop(0, n_valid // 2, pair, 0)

        @pl.when(n_valid % 2 == 1)
        def _last():
            stage(0, n_valid - 1, True)
            consume(0, n_valid - 1, True)


def _silu(z):
    return z * jax.nn.sigmoid(z)


def _causal_steps(n_q_blocks):
    qi = [i for i in range(n_q_blocks) for _ in range(i // SUB + 1)]
    kj = [j for i in range(n_q_blocks) for j in range(i // SUB + 1)]
    return jnp.asarray(qi, I32), jnp.asarray(kj, I32)


def _q_blk(s, qi_ref, kj_ref):
    return qi_ref[s]


def _k_blk(s, qi_ref, kj_ref):
    return kj_ref[s]


def _score_chunk(ik_ref, iqT_ref, w, kc, causal_limit, plane_scr, eq_scr, sel_scr):
    kt = ik_ref[pl.ds(pl.multiple_of(kc * KC, KC), KC), :]
    sc = None
    for h in range(IDX_HEADS):
        dots = _dot(kt, iqT_ref[h * IDX_DIM:(h + 1) * IDX_DIM, :])
        term = w[h:h + 1, :] * jnp.maximum(dots, 0.0)
        sc = term if sc is None else sc + term
    sc = jnp.where(sc == 0.0, 0.0, sc)
    key = _order_key(lax.bitcast_convert_type(sc, I32))
    if causal_limit is not None:
        krow = lax.broadcasted_iota(I32, (KC, TQ), 0)
        qcol = lax.broadcasted_iota(I32, (KC, TQ), 1)
        key = jnp.where(krow <= causal_limit + qcol, key, 0)
    key = key.reshape(WORD_BITS, F32_ROWS, TQ)
    planes = _bit_transpose32([key[r] for r in range(WORD_BITS)])
    for r in range(WORD_BITS):
        plane_scr[kc, r] = planes[r]
    eq_scr[kc] = jnp.full((F32_ROWS, TQ), -1, I32)
    sel_scr[kc] = jnp.zeros((F32_ROWS, TQ), I32)


def _radix_select(i, plane_scr, eq_scr, sel_scr, sel_ref, topk, seq_len):
    n_chunks = sel_ref.shape[1]
    n_groups = (i + RADIX_GROUP) // RADIX_GROUP

    def clear_chunk(c, carry):
        plane_scr[c] = jnp.zeros((WORD_BITS, F32_ROWS, TQ), I32)
        eq_scr[c] = jnp.zeros((F32_ROWS, TQ), I32)
        sel_scr[c] = jnp.zeros((F32_ROWS, TQ), I32)
        return carry

    lax.fori_loop(i + 1, n_groups * RADIX_GROUP, clear_chunk, 0)

    def count_bits(word_fn):
        def body(g, acc):
            for u in range(RADIX_GROUP):
                acc = acc + lax.population_count(word_fn(g * RADIX_GROUP + u))
            return acc
        acc = lax.fori_loop(0, n_groups, body, jnp.zeros((F32_ROWS, TQ), I32))
        return jnp.sum(acc, axis=0, keepdims=True)

    def apply_decision(c, b_prev, drop):
        e = eq_scr[c]
        p = plane_scr[c, b_prev]
        sel_scr[c] = sel_scr[c] | (e & p & drop)
        e = e & (p ^ drop)
        eq_scr[c] = e
        return e

    def decide(cnt_gt, ones):
        take = cnt_gt + ones >= topk
        return jnp.where(take, cnt_gt, cnt_gt + ones), jnp.where(take, 0, -1).astype(I32)

    def radix_step(b, carry):
        cnt_gt, drop = carry
        ones = count_bits(lambda c: apply_decision(c, b - 1, drop) & plane_scr[c, b])
        return decide(cnt_gt, ones)

    first = decide(jnp.zeros((1, TQ), I32), count_bits(lambda c: plane_scr[c, 0]))
    cnt_gt, drop = lax.fori_loop(1, WORD_BITS, radix_step, first)
    cnt_eq = count_bits(lambda c: apply_decision(c, WORD_BITS - 1, drop))

    n_causal = i * TQ + 1 + lax.broadcasted_iota(I32, (1, TQ), 1)
    full = n_causal >= topk
    need = topk - cnt_gt
    trim = full & (cnt_eq > need)
    jx0 = jnp.where(full, seq_len, 0).astype(I32)

    def tie_search(_):
        def locate(c, carry):
            seen, chunk, before = carry
            here = jnp.sum(lax.population_count(eq_scr[c]), axis=0, keepdims=True)
            hit = (seen < need) & (seen + here >= need)
            return seen + here, jnp.where(hit, c, chunk), jnp.where(hit, seen, before)

        zero = jnp.zeros((1, TQ), I32)
        _, chunk, before = lax.fori_loop(0, i + 1, locate, (zero, zero, zero))

        def pick(c, words):
            return words | jnp.where(chunk == c, eq_scr[c], 0)

        words = lax.fori_loop(0, i + 1, pick, jnp.zeros((F32_ROWS, TQ), I32))

        row = zero
        for b in reversed(range(max(1, (KC - 1).bit_length()))):
            cand = row | (1 << b)
            cnt = jnp.sum(lax.population_count(words & _rows_below(cand)), axis=0, keepdims=True)
            row = jnp.where(before + cnt < need, cand, row)
        return jnp.where(trim, chunk * KC + row + 1, jx0)

    jx = lax.cond(jnp.max(trim.astype(I32)) > 0, tie_search, lambda _: jx0, 0)

    def emit(c, carry):
        sel_ref[0, c] = sel_scr[c] | (eq_scr[c] & _rows_below(jx - c * KC))
        return carry

    lax.fori_loop(0, i + 1, emit, 0)

    def emit_empty(c, carry):
        sel_ref[0, c] = jnp.zeros((F32_ROWS, TQ), I32)
        return carry

    lax.fori_loop(i + 1, n_chunks, emit_empty, 0)


def _diff_kernel(qi_ref, kj_ref, qT_ref, k_ref, vT_ref, z_ref, bias_ref, lam_ref, subw_ref,
                 iqT_ref, iwT_ref, ik_ref, y_ref, sel_ref,
                 m_scr, acc_scr, s_scr, smax_scr, plane_scr, eq_scr, sel_scr, *, lam_init, topk, seq_len):
    i = qi_ref[pl.program_id(0)]
    j = kj_ref[pl.program_id(0)]
    d = i - j * SUB

    @pl.when(j == 0)
    def _init():
        m_scr[...] = jnp.full(m_scr.shape, NEG, F32)
        acc_scr[...] = jnp.zeros(acc_scr.shape, F32)

    w = iwT_ref[...] * (IDX_HEADS ** -0.5 * IDX_DIM ** -0.5)

    def stage(buf, c, edge):
        extra = None
        if edge:
            krow = lax.broadcasted_iota(I32, (KC, TQ), 0)
            qcol = lax.broadcasted_iota(I32, (KC, TQ), 1)
            addmask = jnp.where(krow <= (d - c) * KC + qcol, 0.0, NEG)
            bidx = jnp.minimum(d - c, 2)
            extra = lambda mi: bias_ref[bidx, mi // 2] + addmask
        _stage_logits(k_ref, qT_ref, s_scr, smax_scr, buf, c, B_MAPS, B_QK_DIM, extra)

    def consume(buf, c, edge):
        for mi in range(B_MAPS):
            hb = mi // 2
            _softmax_step(s_scr, smax_scr, buf, mi, m_scr, acc_scr,
                          vT_ref[c, hb * B_V_DIM:(hb + 1) * B_V_DIM, :])
        _score_chunk(ik_ref, iqT_ref, w, j * SUB + c, (d - c) * KC if edge else None,
                     plane_scr, eq_scr, sel_scr)

    _attend_step(d, stage, consume)

    @pl.when(j == i // SUB)
    def _finish():
        lq = lam_ref[...]
        lam = (jnp.exp(jnp.sum(lq[0:1, :] * lq[1:2, :], axis=1, keepdims=True))
               - jnp.exp(jnp.sum(lq[2:3, :] * lq[3:4, :], axis=1, keepdims=True)) + lam_init)
        outs = []
        for hb in range(B_HEADS):
            a1 = acc_scr[2 * hb]
            a2 = acc_scr[2 * hb + 1]
            o = (a1[:B_V_DIM] * (1.0 / a1[B_V_DIM:B_V_DIM + 1])
                 - lam * (a2[:B_V_DIM] * (1.0 / a2[B_V_DIM:B_V_DIM + 1])))
            ms = jnp.mean(o * o, axis=0, keepdims=True)
            outs.append(o * lax.rsqrt(ms + RMS_EPS) * subw_ref[...] * (1.0 - lam_init))
        o = jnp.concatenate(outs, axis=0).T
        y_ref[...] = (o * _silu(z_ref[...])).astype(y_ref.dtype)
        _radix_select(i, plane_scr, eq_scr, sel_scr, sel_ref, topk, seq_len)


def _diff_mixer(pr, bias_tiles, lambda_qk, subln_w, lam_init, L):
    topk = min(TOPK_MAX, L // 4)
    n_chunks = L // KC
    qi, kj = _causal_steps(L // TQ)
    grid_spec = pltpu.PrefetchScalarGridSpec(
        num_scalar_prefetch=2,
        grid=(qi.shape[0],),
        in_specs=[
            pl.BlockSpec((B_WIDTH, TQ), lambda *s: (0, _q_blk(*s))),
            pl.BlockSpec((TK, B_WIDTH), lambda *s: (_k_blk(*s), 0)),
            pl.BlockSpec((SUB, B_WIDTH, KC), lambda *s: (_k_blk(*s), 0, 0)),
            pl.BlockSpec((TQ, B_WIDTH), lambda *s: (_q_blk(*s), 0)),
            _resident(bias_tiles.shape),
            _resident(lambda_qk.shape),
            _resident(subln_w.shape),
            pl.BlockSpec((IDX_HEADS * IDX_DIM, TQ), lambda *s: (0, _q_blk(*s))),
            pl.BlockSpec((IDX_HEADS, TQ), lambda *s: (0, _q_blk(*s))),
            _resident((L, IDX_DIM)),
        ],
        out_specs=[
            pl.BlockSpec((TQ, B_WIDTH), lambda *s: (_q_blk(*s), 0)),
            pl.BlockSpec((1, n_chunks, F32_ROWS, TQ), lambda *s: (_q_blk(*s), 0, 0, 0)),
        ],
        scratch_shapes=[
            pltpu.VMEM((B_MAPS, 1, TQ), F32),
            pltpu.VMEM((B_MAPS, B_V_DIM + F32_ROWS, TQ), F32),
            pltpu.VMEM((2, B_MAPS, KC, TQ), F32),
            pltpu.VMEM((2, B_MAPS, 1, TQ), F32),
            pltpu.VMEM((n_chunks, WORD_BITS, F32_ROWS, TQ), I32),
            pltpu.VMEM((n_chunks, F32_ROWS, TQ), I32),
            pltpu.VMEM((n_chunks, F32_ROWS, TQ), I32),
        ],
    )
    return pl.pallas_call(
        functools.partial(_diff_kernel, lam_init=lam_init, topk=topk, seq_len=L),
        grid_spec=grid_spec,
        out_shape=[jax.ShapeDtypeStruct((L, B_WIDTH), BF16),
                   jax.ShapeDtypeStruct((L // TQ, n_chunks, F32_ROWS, TQ), I32)],
        compiler_params=pltpu.CompilerParams(dimension_semantics=("arbitrary",),
                                             vmem_limit_bytes=VMEM_LIMIT),
        name="diff",
    )(qi, kj, pr["b_qT"], pr["b_k"], pr["b_vT"], pr["b_z"], bias_tiles, lambda_qk, subln_w,
      pr["i_qT"], pr["i_wT"], pr["i_k"])


def _dsa_kernel(qi_ref, kj_ref, qT_ref, k_ref, vT_ref, z_ref, bias_ref, sel_ref, y_ref,
                m_scr, acc_scr, s_scr, smax_scr):
    i = qi_ref[pl.program_id(0)]
    j = kj_ref[pl.program_id(0)]
    d = i - j * SUB

    @pl.when(j == 0)
    def _init():
        m_scr[...] = jnp.full(m_scr.shape, NEG, F32)
        acc_scr[...] = jnp.zeros(acc_scr.shape, F32)

    def stage(buf, c, edge):
        sel = sel_ref[0, c]
        addmask = jnp.concatenate(
            [jnp.where((sel << r) < 0, 0.0, NEG) for r in range(WORD_BITS)], axis=0)
        if edge:
            bidx = jnp.minimum(d - c, 2)
            extra = lambda h: bias_ref[bidx, h] + addmask
        else:
            extra = lambda h: addmask
        _stage_logits(k_ref, qT_ref, s_scr, smax_scr, buf, c, A_HEADS, A_HEAD_DIM, extra)

    def consume(buf, c, edge):
        for h in range(A_HEADS):
            lo = h * A_HEAD_DIM
            _softmax_step(s_scr, smax_scr, buf, h, m_scr, acc_scr, vT_ref[c, lo:lo + A_HEAD_DIM, :])

    _attend_step(d, stage, consume)

    @pl.when(j == i // SUB)
    def _finish():
        outs = []
        for h in range(A_HEADS):
            acc = acc_scr[h]
            outs.append(acc[:A_HEAD_DIM] * (1.0 / acc[A_HEAD_DIM:A_HEAD_DIM + 1]))
        o = jnp.concatenate(outs, axis=0).T
        y_ref[...] = (o * _silu(z_ref[...])).astype(y_ref.dtype)


def _dsa_mixer(pr, bias_tiles, sel_bits, L):
    qi, kj = _causal_steps(L // TQ)
    grid_spec = pltpu.PrefetchScalarGridSpec(
        num_scalar_prefetch=2,
        grid=(qi.shape[0],),
        in_specs=[
            pl.BlockSpec((A_WIDTH, TQ), lambda *s: (0, _q_blk(*s))),
            pl.BlockSpec((TK, A_WIDTH), lambda *s: (_k_blk(*s), 0)),
            pl.BlockSpec((SUB, A_WIDTH, KC), lambda *s: (_k_blk(*s), 0, 0)),
            pl.BlockSpec((TQ, A_WIDTH), lambda *s: (_q_blk(*s), 0)),
            _resident(bias_tiles.shape),
            pl.BlockSpec((1, SUB, F32_ROWS, TQ), lambda *s: (_q_blk(*s), _k_blk(*s), 0, 0)),
        ],
        out_specs=pl.BlockSpec((TQ, A_WIDTH), lambda *s: (_q_blk(*s), 0)),
        scratch_shapes=[
            pltpu.VMEM((A_HEADS, 1, TQ), F32),
            pltpu.VMEM((A_HEADS, A_HEAD_DIM + F32_ROWS, TQ), F32),
            pltpu.VMEM((2, A_HEADS, KC, TQ), F32),
            pltpu.VMEM((2, A_HEADS, 1, TQ), F32),
        ],
    )
    return pl.pallas_call(
        _dsa_kernel,
        grid_spec=grid_spec,
        out_shape=jax.ShapeDtypeStruct((L, A_WIDTH), BF16),
        compiler_params=pltpu.CompilerParams(dimension_semantics=("arbitrary",),
                                             vmem_limit_bytes=VMEM_LIMIT),
        name="dsa",
    )(qi, kj, pr["a_qT"], pr["a_k"], pr["a_vT"], pr["a_z"], bias_tiles, sel_bits)


def _merge_kernel(x_ref, p_ref, ya_ref, yb_ref, ga_ref, gb_ref, wpa_ref, wpb_ref, wo_ref,
                  lng_ref, lnb_ref, wple_ref, wgate_ref, out_ref):
    merged = (jax.nn.sigmoid(ga_ref[...]) * _dot(ya_ref[...], wpa_ref[...])
              + jax.nn.sigmoid(gb_ref[...]) * _dot(yb_ref[...], wpb_ref[...]))
    mix = _dot(merged.astype(BF16), wo_ref[...])
    z = DEEPNORM_ALPHA * x_ref[...] + mix
    mu = jnp.mean(z, axis=-1, keepdims=True)
    zc = z - mu
    var = jnp.mean(zc * zc, axis=-1, keepdims=True)
    x1 = zc * lax.rsqrt(var + LN_EPS) * lng_ref[...] + lnb_ref[...]
    gate = jax.nn.sigmoid(_dot(x1.astype(BF16), wgate_ref[...]))
    out_ref[...] = x1 + gate * _dot(p_ref[...].astype(BF16), wple_ref[...])


def _merge(x2, p2, y_a, y_b, g_a, g_b, w_pa, w_pb, w_o, ln_g, ln_b, w_ple, w_gate):
    L = x2.shape[0]
    rows = MERGE_ROWS
    row_blk = lambda width: pl.BlockSpec((rows, width), lambda i: (i, 0))
    weights = (w_pa.astype(BF16), w_pb.astype(BF16), w_o.astype(BF16),
               ln_g.reshape(1, D_MODEL), ln_b.reshape(1, D_MODEL), w_ple.astype(BF16), w_gate.astype(BF16))
    return pl.pallas_call(
        _merge_kernel,
        grid=(L // rows,),
        in_specs=[row_blk(D_MODEL), row_blk(PLE_DIM), row_blk(A_WIDTH), row_blk(B_WIDTH),
                  row_blk(D_MODEL), row_blk(D_MODEL)] + [_resident(a.shape) for a in weights],
        out_specs=row_blk(D_MODEL),
        out_shape=jax.ShapeDtypeStruct((L, D_MODEL), F32),
        compiler_params=pltpu.CompilerParams(dimension_semantics=("arbitrary",),
                                             vmem_limit_bytes=VMEM_LIMIT),
        name="merge",
    )(x2, p2, y_a, y_b, g_a, g_b, *weights)


def kernel(x, p, w_in, w_pa, w_pb, w_o, lambda_qk, subln_w, ln_g, ln_b, w_ple, w_ple_gate, rel_bias):
    b, L, _ = x.shape
    assert b == 1 and w_in.shape[0] == DEPTH == 1
    assert L % TK == 0 and TQ == KC and KC >= MAX_DISTANCE and KC == WORD_BITS * F32_ROWS
    assert (L // KC) % RADIX_GROUP == 0
    tiles = _bias_tiles(rel_bias)
    x2 = x[0]
    pr = _project(x2, w_in[0])
    lam_init = 0.8 - 0.6 * math.exp(-0.3 * 0)
    y_b, sel_bits = _diff_mixer(pr, tiles[:, A_HEADS:], lambda_qk[0], subln_w[0].reshape(B_V_DIM, 1),
                                lam_init, L)
    y_a = _dsa_mixer(pr, tiles[:, :A_HEADS], sel_bits, L)
    out = _merge(x2, p[0, 0], y_a, y_b, pr["g_a"], pr["g_b"], w_pa[0], w_pb[0], w_o[0],
                 ln_g[0], ln_b[0], w_ple[0], w_ple_gate[0])
    return out[None]
```

```python
import functools
import math

import jax
import jax.numpy as jnp
from jax import lax
from jax.experimental import pallas as pl
from jax.experimental.pallas import tpu as pltpu

D_MODEL = 1024
A_HEADS = 8
A_HEAD_DIM = 64
A_WIDTH = A_HEADS * A_HEAD_DIM
IDX_HEADS = 8
IDX_DIM = 64
TOPK_MAX = 256
B_HEADS = 4
B_QK_DIM = 64
B_MAPS = 2 * B_HEADS
B_V_DIM = 2 * B_QK_DIM
B_WIDTH = B_HEADS * B_V_DIM
N_BUCKETS = 32
MAX_DISTANCE = 128
PLE_DIM = 256
LN_EPS = 1e-5
RMS_EPS = 1e-5
DEPTH = 1
DEEPNORM_ALPHA = (2 * DEPTH) ** 0.25
LOG2E = math.log2(math.e)

IN_SIZES = (
    A_WIDTH, A_WIDTH, A_WIDTH, A_WIDTH,
    IDX_HEADS * IDX_DIM, IDX_DIM, IDX_HEADS,
    B_HEADS * 2 * B_QK_DIM, B_HEADS * 2 * B_QK_DIM,
    B_WIDTH, B_WIDTH,
    D_MODEL, D_MODEL,
)

BF16 = jnp.bfloat16
F32 = jnp.float32
I32 = jnp.int32

TQ = 256
KC = 256
SUB = 8
TK = SUB * KC
PROJ_ROWS = KC
MERGE_ROWS = 512
VMEM_LIMIT = 56 * 1024 * 1024
BF16_ROWS = 16
F32_ROWS = 8
WORD_BITS = 32
RADIX_GROUP = 8

NEG = -1e30
INT_MIN = -(2 ** 31)


def _nt_dot(a, b):
    return lax.dot_general(a, b, (((1,), (1,)), ((), ())), preferred_element_type=F32)


def _dot(a, b):
    return jnp.dot(a, b, preferred_element_type=F32)


def _order_key(bits):
    return bits ^ ((bits >> 31) | INT_MIN)


def _bit_transpose32(words):
    w = list(words)
    shift, mask = 16, 0x0000FFFF
    while shift:
        for k in range(WORD_BITS):
            if not k & shift:
                t = (w[k] ^ (w[k + shift] >> shift)) & mask
                w[k] = w[k] ^ t
                w[k + shift] = w[k + shift] ^ (t << shift)
        shift >>= 1
        mask = (mask ^ (mask << shift)) & 0xFFFFFFFF
    return w


def _rows_below(limit):
    sub = lax.broadcasted_iota(I32, (F32_ROWS, TQ), 0)
    n = jnp.clip((limit - sub + (F32_ROWS - 1)) >> 3, 0, WORD_BITS)
    return jnp.where(n > 0, jnp.int32(INT_MIN) >> (jnp.maximum(n, 1) - 1), 0)


def _resident(shape):
    return pl.BlockSpec(shape, lambda *_: (0,) * len(shape), pipeline_mode=pl.Buffered(1))


_OUTS = (
    ("a_k", A_WIDTH, BF16, BF16, False, 1.0, False),
    ("b_k", B_WIDTH, BF16, BF16, False, 1.0, False),
    ("a_z", A_WIDTH, F32, BF16, False, 1.0, False),
    ("b_z", B_WIDTH, F32, BF16, False, 1.0, False),
    ("g_a", D_MODEL, F32, BF16, False, 1.0, False),
    ("g_b", D_MODEL, F32, BF16, False, 1.0, False),
    ("i_k", IDX_DIM, F32, F32, False, 1.0, False),
    ("a_qT", A_WIDTH, BF16, BF16, True, A_HEAD_DIM ** -0.5 * LOG2E, False),
    ("a_vT", A_WIDTH, BF16, BF16, True, 1.0, True),
    ("b_qT", B_WIDTH, BF16, BF16, True, B_QK_DIM ** -0.5 * LOG2E, False),
    ("b_vT", B_WIDTH, BF16, BF16, True, 1.0, True),
    ("i_qT", IDX_HEADS * IDX_DIM, F32, F32, True, 1.0, False),
    ("i_wT", IDX_HEADS, F32, F32, True, 1.0, False),
)
_W_GROUPS = ((BF16, False), (BF16, True), (F32, False), (F32, True))


def _proj_kernel(x_ref, *refs):
    w_refs = dict(zip(_W_GROUPS, refs[:len(_W_GROUPS)]))
    out_refs = refs[len(_W_GROUPS):]
    x = {F32: x_ref[...]}
    x[BF16] = x[F32].astype(BF16)
    off = {g: 0 for g in _W_GROUPS}
    for o_ref, (_, width, dtype, op_dtype, feature_major, scale, chunked) in zip(out_refs, _OUTS):
        g = (op_dtype, feature_major)
        lo = off[g]
        off[g] = lo + width
        if feature_major:
            r = _nt_dot(w_refs[g][lo:lo + width, :], x[op_dtype])
        else:
            r = _dot(x[op_dtype], w_refs[g][:, lo:lo + width])
        if scale != 1.0:
            r = r * scale
        if chunked:
            o_ref[0] = r.astype(dtype)
        else:
            o_ref[...] = r.astype(dtype)


def _project(x2, w):
    L = x2.shape[0]
    pts = [0]
    for s in IN_SIZES:
        pts.append(pts[-1] + s)
    names = ("a_q", "a_k", "a_v", "a_z", "i_q", "i_k", "i_w", "b_q", "b_k", "b_v", "b_z", "g_a", "g_b")
    col = {n: w[:, pts[k]:pts[k + 1]] for k, n in enumerate(names)}
    weights = []
    for op_dtype, feature_major in _W_GROUPS:
        cols = [col[o[0][:-1] if feature_major else o[0]] for o in _OUTS if (o[3], o[4]) == (op_dtype, feature_major)]
        wg = jnp.concatenate(cols, axis=1).astype(op_dtype)
        weights.append(wg.T if feature_major else wg)
    rows = PROJ_ROWS
    out_shape, out_specs = [], []
    for _, width, dtype, _, feature_major, _, chunked in _OUTS:
        if chunked:
            out_shape.append(jax.ShapeDtypeStruct((L // rows, width, rows), dtype))
            out_specs.append(pl.BlockSpec((1, width, rows), lambda i: (i, 0, 0)))
        elif feature_major:
            out_shape.append(jax.ShapeDtypeStruct((width, L), dtype))
            out_specs.append(pl.BlockSpec((width, rows), lambda i: (0, i)))
        else:
            out_shape.append(jax.ShapeDtypeStruct((L, width), dtype))
            out_specs.append(pl.BlockSpec((rows, width), lambda i: (i, 0)))
    outs = pl.pallas_call(
        _proj_kernel,
        grid=(L // rows,),
        in_specs=[pl.BlockSpec((rows, D_MODEL), lambda i: (i, 0))] + [_resident(wg.shape) for wg in weights],
        out_specs=out_specs,
        out_shape=out_shape,
        compiler_params=pltpu.CompilerParams(dimension_semantics=("arbitrary",),
                                             vmem_limit_bytes=VMEM_LIMIT),
        name="proj",
    )(x2, *weights)
    return {o[0]: r for o, r in zip(_OUTS, outs)}


def _rel_bucket(dist):
    n = jnp.maximum(dist, 0)
    max_exact = N_BUCKETS // 2
    nf = jnp.maximum(n, 1).astype(F32)
    large = max_exact + (jnp.log(nf / max_exact) / math.log(MAX_DISTANCE / max_exact)
                         * (N_BUCKETS - max_exact)).astype(I32)
    large = jnp.minimum(large, N_BUCKETS - 1)
    return jnp.where(n < max_exact, n, large)


def _bias_tiles(rel_bias):
    shifted = rel_bias - rel_bias[N_BUCKETS - 1]
    by_dist = shifted[_rel_bucket(jnp.arange(2 * KC + TQ - 1) - (KC - 1))].T
    n_heads = by_dist.shape[0]
    tiles = []
    for d in range(2):
        c = d * KC + KC - 1
        period = jnp.concatenate([by_dist[:, c:c + TQ], jnp.zeros((n_heads, 1), F32),
                                  by_dist[:, c - (KC - 1):c]], axis=1)
        flat = jnp.tile(period, (1, KC))[:, :KC * (KC + TQ - 1)]
        tiles.append(flat.reshape(n_heads, KC, KC + TQ - 1)[:, :, :TQ])
    tiles.append(jnp.zeros_like(tiles[0]))
    return (jnp.stack(tiles) * LOG2E).astype(F32)


def _stage_logits(k_ref, qT_ref, s_scr, smax_scr, buf, c, n_streams, dim, extra=None):
    ks = c * KC if isinstance(c, int) else pl.multiple_of(c * KC, KC)
    for h in range(n_streams):
        lo = h * dim
        s = _dot(k_ref[pl.ds(ks, KC), lo:lo + dim], qT_ref[lo:lo + dim, :])
        if extra is not None:
            s = s + extra(h)
        s_scr[buf, h] = s
        smax_scr[buf, h] = jnp.max(s, axis=0, keepdims=True)


def _softmax_step(s_scr, smax_scr, buf, h, m_ref, acc_ref, vt):
    rows = acc_ref.shape[1]
    m_old = m_ref[h]
    m_new = jnp.maximum(m_old, smax_scr[buf, h])
    alpha = jnp.exp2(m_old - m_new)
    p = jnp.exp2(s_scr[buf, h] - m_new).astype(BF16)
    vt_ext = jnp.concatenate([vt, jnp.ones((BF16_ROWS, KC), BF16)], axis=0)
    acc_ref[h] = alpha * acc_ref[h] + _dot(vt_ext, p)[:rows]
    m_ref[h] = m_new


def _attend_step(d, stage, consume):
    @pl.when(d > SUB)
    def _far_step():
        stage(0, 0, False)
        for c in range(SUB):
            if c + 1 < SUB:
                stage((c + 1) % 2, c + 1, False)
            consume(c % 2, c, False)

    @pl.when((d >= 0) & (d <= SUB))
    def _edge_step():
        n_valid = jnp.minimum(d + 1, SUB)

        def pair(p, carry):
            stage(0, 2 * p, True)
            stage(1, 2 * p + 1, True)
            consume(0, 2 * p, True)
            consume(1, 2 * p + 1, True)
            return carry

        lax.fori_loop(0, n_valid // 2, pair, 0)

        @pl.when(n_valid % 2 == 1)
        def _last():
            stage(0, n_valid - 1, True)
            consume(0, n_valid - 1, True)


def _silu(z):
    return z * jax.nn.sigmoid(z)


def _causal_steps(n_q_blocks):
    qi = [i for i in range(n_q_blocks) for _ in range(i // SUB + 1)]
    kj = [j for i in range(n_q_blocks) for j in range(i // SUB + 1)]
    return jnp.asarray(qi, I32), jnp.asarray(kj, I32)


def _q_blk(s, qi_ref, kj_ref):
    return qi_ref[s]


def _k_blk(s, qi_ref, kj_ref):
    return kj_ref[s]


def _score_chunk(ik_ref, iqT_ref, w, kc, causal_limit, plane_scr, eq_scr, sel_scr):
    kt = ik_ref[pl.ds(pl.multiple_of(kc * KC, KC), KC), :]
    sc = None
    for h in range(IDX_HEADS):
        dots = _dot(kt, iqT_ref[h * IDX_DIM:(h + 1) * IDX_DIM, :])
        term = w[h:h + 1, :] * jnp.maximum(dots, 0.0)
        sc = term if sc is None else sc + term
    sc = jnp.where(sc == 0.0, 0.0, sc)
    key = _order_key(lax.bitcast_convert_type(sc, I32))
    if causal_limit is not None:
        krow = lax.broadcasted_iota(I32, (KC, TQ), 0)
        qcol = lax.broadcasted_iota(I32, (KC, TQ), 1)
        key = jnp.where(krow <= causal_limit + qcol, key, 0)
    key = key.reshape(WORD_BITS, F32_ROWS, TQ)
    planes = _bit_transpose32([key[r] for r in range(WORD_BITS)])
    for r in range(WORD_BITS):
        plane_scr[kc, r] = planes[r]
    eq_scr[kc] = jnp.full((F32_ROWS, TQ), -1, I32)
    sel_scr[kc] = jnp.zeros((F32_ROWS, TQ), I32)


def _radix_select(i, plane_scr, eq_scr, sel_scr, sel_ref, topk, seq_len):
    n_chunks = sel_ref.shape[1]
    n_groups = (i + RADIX_GROUP) // RADIX_GROUP

    def clear_chunk(c, carry):
        plane_scr[c] = jnp.zeros((WORD_BITS, F32_ROWS, TQ), I32)
        eq_scr[c] = jnp.zeros((F32_ROWS, TQ), I32)
        sel_scr[c] = jnp.zeros((F32_ROWS, TQ), I32)
        return carry

    lax.fori_loop(i + 1, n_groups * RADIX_GROUP, clear_chunk, 0)

    def count_bits(word_fn):
        def body(g, acc):
            for u in range(RADIX_GROUP):
                acc = acc + lax.population_count(word_fn(g * RADIX_GROUP + u))
            return acc
        acc = lax.fori_loop(0, n_groups, body, jnp.zeros((F32_ROWS, TQ), I32))
        return jnp.sum(acc, axis=0, keepdims=True)

    def apply_decision(c, b_prev, drop):
        e = eq_scr[c]
        p = plane_scr[c, b_prev]
        sel_scr[c] = sel_scr[c] | (e & p & drop)
        e = e & (p ^ drop)
        eq_scr[c] = e
        return e

    def decide(cnt_gt, ones):
        take = cnt_gt + ones >= topk
        return jnp.where(take, cnt_gt, cnt_gt + ones), jnp.where(take, 0, -1).astype(I32)

    def radix_step(b, carry):
        cnt_gt, drop = carry
        ones = count_bits(lambda c: apply_decision(c, b - 1, drop) & plane_scr[c, b])
        return decide(cnt_gt, ones)

    first = decide(jnp.zeros((1, TQ), I32), count_bits(lambda c: plane_scr[c, 0]))
    cnt_gt, drop = lax.fori_loop(1, WORD_BITS, radix_step, first)
    cnt_eq = count_bits(lambda c: apply_decision(c, WORD_BITS - 1, drop))

    n_causal = i * TQ + 1 + lax.broadcasted_iota(I32, (1, TQ), 1)
    full = n_causal >= topk
    need = topk - cnt_gt
    trim = full & (cnt_eq > need)
    jx0 = jnp.where(full, seq_len, 0).astype(I32)

    def tie_search(_):
        def locate(c, carry):
            seen, chunk, before = carry
            here = jnp.sum(lax.population_count(eq_scr[c]), axis=0, keepdims=True)
            hit = (seen < need) & (seen + here >= need)
            return seen + here, jnp.where(hit, c, chunk), jnp.where(hit, seen, before)

        zero = jnp.zeros((1, TQ), I32)
        _, chunk, before = lax.fori_loop(0, i + 1, locate, (zero, zero, zero))

        def pick(c, words):
            return words | jnp.where(chunk == c, eq_scr[c], 0)

        words = lax.fori_loop(0, i + 1, pick, jnp.zeros((F32_ROWS, TQ), I32))

        row = zero
        for b in reversed(range(max(1, (KC - 1).bit_length()))):
            cand = row | (1 << b)
            cnt = jnp.sum(lax.population_count(words & _rows_below(cand)), axis=0, keepdims=True)
            row = jnp.where(before + cnt < need, cand, row)
        return jnp.where(trim, chunk * KC + row + 1, jx0)

    jx = lax.cond(jnp.max(trim.astype(I32)) > 0, tie_search, lambda _: jx0, 0)

    def emit(c, carry):
        sel_ref[0, c] = sel_scr[c] | (eq_scr[c] & _rows_below(jx - c * KC))
        return carry

    lax.fori_loop(0, i + 1, emit, 0)

    def emit_empty(c, carry):
        sel_ref[0, c] = jnp.zeros((F32_ROWS, TQ), I32)
        return carry

    lax.fori_loop(i + 1, n_chunks, emit_empty, 0)


def _diff_kernel(qi_ref, kj_ref, qT_ref, k_ref, vT_ref, z_ref, bias_ref, lam_ref, subw_ref,
                 iqT_ref, iwT_ref, ik_ref, y_ref, sel_ref,
                 m_scr, acc_scr, s_scr, smax_scr, plane_scr, eq_scr, sel_scr, *, lam_init, topk, seq_len):
    i = qi_ref[pl.program_id(0)]
    j = kj_ref[pl.program_id(0)]
    d = i - j * SUB

    @pl.when(j == 0)
    def _init():
        m_scr[...] = jnp.full(m_scr.shape, NEG, F32)
        acc_scr[...] = jnp.zeros(acc_scr.shape, F32)

    w = iwT_ref[...] * (IDX_HEADS ** -0.5 * IDX_DIM ** -0.5)

    def stage(buf, c, edge):
        extra = None
        if edge:
            krow = lax.broadcasted_iota(I32, (KC, TQ), 0)
            qcol = lax.broadcasted_iota(I32, (KC, TQ), 1)
            addmask = jnp.where(krow <= (d - c) * KC + qcol, 0.0, NEG)
            bidx = jnp.minimum(d - c, 2)
            extra = lambda mi: bias_ref[bidx, mi // 2] + addmask
        _stage_logits(k_ref, qT_ref, s_scr, smax_scr, buf, c, B_MAPS, B_QK_DIM, extra)

    def consume(buf, c, edge):
        for mi in range(B_MAPS):
            hb = mi // 2
            _softmax_step(s_scr, smax_scr, buf, mi, m_scr, acc_scr,
                          vT_ref[c, hb * B_V_DIM:(hb + 1) * B_V_DIM, :])
        _score_chunk(ik_ref, iqT_ref, w, j * SUB + c, (d - c) * KC if edge else None,
                     plane_scr, eq_scr, sel_scr)

    _attend_step(d, stage, consume)

    @pl.when(j == i // SUB)
    def _finish():
        lq = lam_ref[...]
        lam = (jnp.exp(jnp.sum(lq[0:1, :] * lq[1:2, :], axis=1, keepdims=True))
               - jnp.exp(jnp.sum(lq[2:3, :] * lq[3:4, :], axis=1, keepdims=True)) + lam_init)
        outs = []
        for hb in range(B_HEADS):
            a1 = acc_scr[2 * hb]
            a2 = acc_scr[2 * hb + 1]
            o = (a1[:B_V_DIM] * (1.0 / a1[B_V_DIM:B_V_DIM + 1])
                 - lam * (a2[:B_V_DIM] * (1.0 / a2[B_V_DIM:B_V_DIM + 1])))
            ms = jnp.mean(o * o, axis=0, keepdims=True)
            outs.append(o * lax.rsqrt(ms + RMS_EPS) * subw_ref[...] * (1.0 - lam_init))
        o = jnp.concatenate(outs, axis=0).T
        y_ref[...] = (o * _silu(z_ref[...])).astype(y_ref.dtype)
        _radix_select(i, plane_scr, eq_scr, sel_scr, sel_ref, topk, seq_len)


def _diff_mixer(pr, bias_tiles, lambda_qk, subln_w, lam_init, L):
    topk = min(TOPK_MAX, L // 4)
    n_chunks = L // KC
    qi, kj = _causal_steps(L // TQ)
    grid_spec = pltpu.PrefetchScalarGridSpec(
        num_scalar_prefetch=2,
        grid=(qi.shape[0],),
        in_specs=[
            pl.BlockSpec((B_WIDTH, TQ), lambda *s: (0, _q_blk(*s))),
            pl.BlockSpec((TK, B_WIDTH), lambda *s: (_k_blk(*s), 0)),
            pl.BlockSpec((SUB, B_WIDTH, KC), lambda *s: (_k_blk(*s), 0, 0)),
            pl.BlockSpec((TQ, B_WIDTH), lambda *s: (_q_blk(*s), 0)),
            _resident(bias_tiles.shape),
            _resident(lambda_qk.shape),
            _resident(subln_w.shape),
            pl.BlockSpec((IDX_HEADS * IDX_DIM, TQ), lambda *s: (0, _q_blk(*s))),
            pl.BlockSpec((IDX_HEADS, TQ), lambda *s: (0, _q_blk(*s))),
            _resident((L, IDX_DIM)),
        ],
        out_specs=[
            pl.BlockSpec((TQ, B_WIDTH), lambda *s: (_q_blk(*s), 0)),
            pl.BlockSpec((1, n_chunks, F32_ROWS, TQ), lambda *s: (_q_blk(*s), 0, 0, 0)),
        ],
        scratch_shapes=[
            pltpu.VMEM((B_MAPS, 1, TQ), F32),
            pltpu.VMEM((B_MAPS, B_V_DIM + F32_ROWS, TQ), F32),
            pltpu.VMEM((2, B_MAPS, KC, TQ), F32),
            pltpu.VMEM((2, B_MAPS, 1, TQ), F32),
            pltpu.VMEM((n_chunks, WORD_BITS, F32_ROWS, TQ), I32),
            pltpu.VMEM((n_chunks, F32_ROWS, TQ), I32),
            pltpu.VMEM((n_chunks, F32_ROWS, TQ), I32),
        ],
    )
    return pl.pallas_call(
        functools.partial(_diff_kernel, lam_init=lam_init, topk=topk, seq_len=L),
        grid_spec=grid_spec,
        out_shape=[jax.ShapeDtypeStruct((L, B_WIDTH), BF16),
                   jax.ShapeDtypeStruct((L // TQ, n_chunks, F32_ROWS, TQ), I32)],
        compiler_params=pltpu.CompilerParams(dimension_semantics=("arbitrary",),
                                             vmem_limit_bytes=VMEM_LIMIT),
        name="diff",
    )(qi, kj, pr["b_qT"], pr["b_k"], pr["b_vT"], pr["b_z"], bias_tiles, lambda_qk, subln_w,
      pr["i_qT"], pr["i_wT"], pr["i_k"])


def _dsa_kernel(qi_ref, kj_ref, qT_ref, k_ref, vT_ref, z_ref, bias_ref, sel_ref, y_ref,
                m_scr, acc_scr, s_scr, smax_scr):
    i = qi_ref[pl.program_id(0)]
    j = kj_ref[pl.program_id(0)]
    d = i - j * SUB

    @pl.when(j == 0)
    def _init():
        m_scr[...] = jnp.full(m_scr.shape, NEG, F32)
        acc_scr[...] = jnp.zeros(acc_scr.shape, F32)

    def stage(buf, c, edge):
        sel = sel_ref[0, c]
        addmask = jnp.concatenate(
            [jnp.where((sel << r) < 0, 0.0, NEG) for r in range(WORD_BITS)], axis=0)
        if edge:
            bidx = jnp.minimum(d - c, 2)
            extra = lambda h: bias_ref[bidx, h] + addmask
        else:
            extra = lambda h: addmask
        _stage_logits(k_ref, qT_ref, s_scr, smax_scr, buf, c, A_HEADS, A_HEAD_DIM, extra)

    def consume(buf, c, edge):
        for h in range(A_HEADS):
            lo = h * A_HEAD_DIM
            _softmax_step(s_scr, smax_scr, buf, h, m_scr, acc_scr, vT_ref[c, lo:lo + A_HEAD_DIM, :])

    _attend_step(d, stage, consume)

    @pl.when(j == i // SUB)
    def _finish():
        outs = []
        for h in range(A_HEADS):
            acc = acc_scr[h]
            outs.append(acc[:A_HEAD_DIM] * (1.0 / acc[A_HEAD_DIM:A_HEAD_DIM + 1]))
        o = jnp.concatenate(outs, axis=0).T
        y_ref[...] = (o * _silu(z_ref[...])).astype(y_ref.dtype)


def _dsa_mixer(pr, bias_tiles, sel_bits, L):
    qi, kj = _causal_steps(L // TQ)
    grid_spec = pltpu.PrefetchScalarGridSpec(
        num_scalar_prefetch=2,
        grid=(qi.shape[0],),
        in_specs=[
            pl.BlockSpec((A_WIDTH, TQ), lambda *s: (0, _q_blk(*s))),
            pl.BlockSpec((TK, A_WIDTH), lambda *s: (_k_blk(*s), 0)),
            pl.BlockSpec((SUB, A_WIDTH, KC), lambda *s: (_k_blk(*s), 0, 0)),
            pl.BlockSpec((TQ, A_WIDTH), lambda *s: (_q_blk(*s), 0)),
            _resident(bias_tiles.shape),
            pl.BlockSpec((1, SUB, F32_ROWS, TQ), lambda *s: (_q_blk(*s), _k_blk(*s), 0, 0)),
        ],
        out_specs=pl.BlockSpec((TQ, A_WIDTH), lambda *s: (_q_blk(*s), 0)),
        scratch_shapes=[
            pltpu.VMEM((A_HEADS, 1, TQ), F32),
            pltpu.VMEM((A_HEADS, A_HEAD_DIM + F32_ROWS, TQ), F32),
            pltpu.VMEM((2, A_HEADS, KC, TQ), F32),
            pltpu.VMEM((2, A_HEADS, 1, TQ), F32),
        ],
    )
    return pl.pallas_call(
        _dsa_kernel,
        grid_spec=grid_spec,
        out_shape=jax.ShapeDtypeStruct((L, A_WIDTH), BF16),
        compiler_params=pltpu.CompilerParams(dimension_semantics=("arbitrary",),
                                             vmem_limit_bytes=VMEM_LIMIT),
        name="dsa",
    )(qi, kj, pr["a_qT"], pr["a_k"], pr["a_vT"], pr["a_z"], bias_tiles, sel_bits)


def _merge_kernel(x_ref, p_ref, ya_ref, yb_ref, ga_ref, gb_ref, wpa_ref, wpb_ref, wo_ref,
                  lng_ref, lnb_ref, wple_ref, wgate_ref, out_ref):
    merged = (jax.nn.sigmoid(ga_ref[...]) * _dot(ya_ref[...], wpa_ref[...])
              + jax.nn.sigmoid(gb_ref[...]) * _dot(yb_ref[...], wpb_ref[...]))
    mix = _dot(merged.astype(BF16), wo_ref[...])
    z = DEEPNORM_ALPHA * x_ref[...] + mix
    mu = jnp.mean(z, axis=-1, keepdims=True)
    zc = z - mu
    var = jnp.mean(zc * zc, axis=-1, keepdims=True)
    x1 = zc * lax.rsqrt(var + LN_EPS) * lng_ref[...] + lnb_ref[...]
    gate = jax.nn.sigmoid(_dot(x1.astype(BF16), wgate_ref[...]))
    out_ref[...] = x1 + gate * _dot(p_ref[...].astype(BF16), wple_ref[...])


def _merge(x2, p2, y_a, y_b, g_a, g_b, w_pa, w_pb, w_o, ln_g, ln_b, w_ple, w_gate):
    L = x2.shape[0]
    rows = MERGE_ROWS
    row_blk = lambda width: pl.BlockSpec((rows, width), lambda i: (i, 0))
    weights = (w_pa.astype(BF16), w_pb.astype(BF16), w_o.astype(BF16),
               ln_g.reshape(1, D_MODEL), ln_b.reshape(1, D_MODEL), w_ple.astype(BF16), w_gate.astype(BF16))
    return pl.pallas_call(
        _merge_kernel,
        grid=(L // rows,),
        in_specs=[row_blk(D_MODEL), row_blk(PLE_DIM), row_blk(A_WIDTH), row_blk(B_WIDTH),
                  row_blk(D_MODEL), row_blk(D_MODEL)] + [_resident(a.shape) for a in weights],
        out_specs=row_blk(D_MODEL),
        out_shape=jax.ShapeDtypeStruct((L, D_MODEL), F32),
        compiler_params=pltpu.CompilerParams(dimension_semantics=("arbitrary",),
                                             vmem_limit_bytes=VMEM_LIMIT),
        name="merge",
    )(x2, p2, y_a, y_b, g_a, g_b, *weights)


def kernel(x, p, w_in, w_pa, w_pb, w_o, lambda_qk, subln_w, ln_g, ln_b, w_ple, w_ple_gate, rel_bias):
    b, L, _ = x.shape
    assert b == 1 and w_in.shape[0] == DEPTH == 1
    assert L % TK == 0 and TQ == KC and KC >= MAX_DISTANCE and KC == WORD_BITS * F32_ROWS
    assert (L // KC) % RADIX_GROUP == 0
    tiles = _bias_tiles(rel_bias)
    x2 = x[0]
    pr = _project(x2, w_in[0])
    lam_init = 0.8 - 0.6 * math.exp(-0.3 * 0)
    y_b, sel_bits = _diff_mixer(pr, tiles[:, A_HEADS:], lambda_qk[0], subln_w[0].reshape(B_V_DIM, 1),
                                lam_init, L)
    y_a = _dsa_mixer(pr, tiles[:, :A_HEADS], sel_bits, L)
    out = _merge(x2, p[0, 0], y_a, y_b, pr["g_a"], pr["g_b"], w_pa[0], w_pb[0], w_o[0],
                 ln_g[0], ln_b[0], w_ple[0], w_ple_gate[0])
    return out[None]
```
